```python
import jax, jax.numpy as jnp
from jax import lax
import numpy as np

D_MODEL = 1024
BATCH = 2
SEQ = 8192
DEPTH = 1
DEC_BATCH = 4
DEC_SEQ = 4096
PAST_LEN = 128

GRID_W = 64
HEAD_DIM = 64
ATTN_HEADS = 8
ATTN_KV_HEADS = 2
ATTN_GROUP = ATTN_HEADS // ATTN_KV_HEADS
ATTN_WIDTH = ATTN_HEADS * HEAD_DIM
KV_WIDTH = ATTN_KV_HEADS * HEAD_DIM
Q_BLOCK = 128
ROPE_THETA = 10000.0
RWKV_HEADS = 8
RWKV_HEAD = 64
RWKV_WIDTH = RWKV_HEADS * RWKV_HEAD
DECAY_LORA = 64
AAA_LORA = 64
GATE_LORA = 128
N_DIR = 2
LNX_EPS = 64e-5
N_BRANCH = 2
D_FF = ((-(-8 * D_MODEL // 3) + 255) // 256) * 256
NORM_EPS = 1e-6

OFF_Q = 0
OFF_K = OFF_Q + ATTN_WIDTH
OFF_V = OFF_K + KV_WIDTH
OFF_R = OFF_V + KV_WIDTH
OFF_RK = OFF_R + RWKV_WIDTH
OFF_RV = OFF_RK + RWKV_WIDTH
OFF_WD = OFF_RV + RWKV_WIDTH
OFF_AD = OFF_WD + N_DIR * DECAY_LORA
OFF_GD = OFF_AD + N_DIR * AAA_LORA
OFF_GATE = OFF_GD + GATE_LORA
IN_COLS = OFF_GATE + N_BRANCH * D_MODEL
SHIFT_COLS = OFF_GATE - OFF_R

kernel_name = "hybrid_axial_gqa_rwkv7_bidir_encoder"


def rms_norm(x, g, eps=NORM_EPS):
    xf = x.astype(jnp.float32)
    y = xf * lax.rsqrt(jnp.mean(xf * xf, axis=-1, keepdims=True) + eps)
    return (y * g.astype(jnp.float32)).astype(x.dtype)


def axial_rope_tables(n):
    rows = n // GRID_W
    row = jnp.broadcast_to(jnp.arange(rows)[:, None], (rows, GRID_W)).reshape(-1).astype(jnp.float32)
    col = jnp.broadcast_to(jnp.arange(GRID_W)[None, :], (rows, GRID_W)).reshape(-1).astype(jnp.float32)
    nf = HEAD_DIM // 4
    inv = ROPE_THETA ** (-jnp.arange(nf, dtype=jnp.float32) / nf)
    ang = jnp.concatenate([row[:, None] * inv, col[:, None] * inv], axis=-1)
    return jnp.cos(ang), jnp.sin(ang)


def apply_rope(x, cos, sin):
    xf = x.astype(jnp.float32).reshape(x.shape[:-1] + (HEAD_DIM // 2, 2))
    x1, x2 = xf[..., 0], xf[..., 1]
    c = cos[None, :, None, :]
    s = sin[None, :, None, :]
    out = jnp.stack([x1 * c - x2 * s, x1 * s + x2 * c], axis=-1)
    return out.reshape(x.shape).astype(x.dtype)


def axial_gqa_attention(q, k, v, q_gain, k_gain):
    b, n = q.shape[:2]
    q = rms_norm(q.reshape(b, n, ATTN_HEADS, HEAD_DIM), q_gain)
    k = rms_norm(k.reshape(b, n, ATTN_KV_HEADS, HEAD_DIM), k_gain)
    v = v.reshape(b, n, ATTN_KV_HEADS, HEAD_DIM)
    cos, sin = axial_rope_tables(n)
    q = apply_rope(q, cos, sin)
    k = apply_rope(k, cos, sin)
    nblk = n // Q_BLOCK
    qb = q.reshape(b, nblk, Q_BLOCK, ATTN_KV_HEADS, ATTN_GROUP, HEAD_DIM).transpose(1, 0, 2, 3, 4, 5)
    scale = HEAD_DIM ** -0.5

    def one_block(qi):
        s = jnp.einsum('bqhgd,bkhd->bhgqk', qi, k, preferred_element_type=jnp.float32) * scale
        p = jax.nn.softmax(s, axis=-1).astype(v.dtype)
        return jnp.einsum('bhgqk,bkhd->bqhgd', p, v)

    o = lax.map(one_block, qb)
    return o.transpose(1, 0, 2, 3, 4, 5).reshape(b, n, ATTN_WIDTH)


def centred_shift_mix(p, mu):
    prev = jnp.pad(p[:, :-1], ((0, 0), (1, 0), (0, 0)))
    nxt = jnp.pad(p[:, 1:], ((0, 0), (0, 1), (0, 0)))
    return p + (0.5 * (prev + nxt) - p) * mu


def rwkv7_scan(r, decay, k, v, a_vec, b_vec, reverse):
    b, n, h, d = r.shape

    def step(S, inp):
        r_t, w_t, k_t, v_t, a_t, b_t = inp
        sa = jnp.einsum('bhvk,bhk->bhv', S, a_t)
        S = S * w_t[:, :, None, :] + sa[..., None] * b_t[:, :, None, :] + v_t[..., None] * k_t[:, :, None, :]
        return S, jnp.einsum('bhvk,bhk->bhv', S, r_t)

    xs = tuple(jnp.swapaxes(t, 0, 1) for t in (r, decay, k, v, a_vec, b_vec))
    s0 = jnp.zeros((b, h, d, d), jnp.float32)
    _, ys = lax.scan(step, s0, xs, reverse=reverse)
    return jnp.swapaxes(ys, 0, 1)


def rwkv7_bidirectional(z, w0, w_up, a0, a_up, g_up, k_k, k_a, r_k, lnx_w, lnx_b):
    f32 = jnp.float32
    b, n = z.shape[:2]
    hs = lambda t: t.reshape(b, n, RWKV_HEADS, RWKV_HEAD)
    r = z[..., 0:RWKV_WIDTH]
    k = z[..., RWKV_WIDTH:2 * RWKV_WIDTH]
    v = z[..., 2 * RWKV_WIDTH:3 * RWKV_WIDTH]
    o_wd = OFF_WD - OFF_R
    o_ad = OFF_AD - OFF_R
    o_gd = OFF_GD - OFF_R
    wd = z[..., o_wd:o_ad].reshape(b, n, N_DIR, DECAY_LORA)
    ad = z[..., o_ad:o_gd].reshape(b, n, N_DIR, AAA_LORA)
    gd = z[..., o_gd:]
    w_log = -jax.nn.softplus(-(w0.astype(f32) + jnp.einsum('btdr,drc->btdc', jnp.tanh(wd), w_up.astype(f32)))) - 0.5
    decay = jnp.exp(-jnp.exp(w_log))
    a = jax.nn.sigmoid(a0.astype(f32) + jnp.einsum('btdr,drc->btdc', ad, a_up.astype(f32)))
    g = jax.nn.sigmoid(gd) @ g_up.astype(f32)
    kk = hs(k * k_k.astype(f32))
    kk = kk / jnp.maximum(jnp.sqrt(jnp.sum(kk * kk, axis=-1, keepdims=True)), 1e-12)
    k_dir = k[:, :, None, :] * (1.0 + (a - 1.0) * k_a.astype(f32))
    y = jnp.zeros((b, n, RWKV_HEADS, RWKV_HEAD), f32)
    for d, rev in enumerate((False, True)):
        y = y + rwkv7_scan(hs(r), hs(decay[:, :, d]), hs(k_dir[:, :, d]), hs(v), -kk, kk * hs(a[:, :, d]), rev)
    mu = jnp.mean(y, axis=-1, keepdims=True)
    var = jnp.mean(jnp.square(y - mu), axis=-1, keepdims=True)
    y = ((y - mu) * lax.rsqrt(var + LNX_EPS)).reshape(b, n, RWKV_WIDTH) * lnx_w.astype(f32) + lnx_b.astype(f32)
    k_bonus = hs(0.5 * (k_dir[:, :, 0] + k_dir[:, :, 1]))
    bonus = jnp.sum(hs(r) * k_bonus * r_k.astype(f32), axis=-1, keepdims=True) * hs(v)
    return (y + bonus.reshape(b, n, RWKV_WIDTH)) * g


def encoder_layer(x, ln_mix_pre, ln_mix_post, ln_ffn_pre, ln_ffn_post, w_in, q_gain, k_gain, shift_mu,
                  w0, w_up, a0, a_up, g_up, k_k, k_a, r_k, lnx_w, lnx_b, w_branch, w_out, w_ffn_in, w_ffn_out):
    dt = x.dtype
    b, n = x.shape[:2]
    h = rms_norm(x, ln_mix_pre)
    u = h @ w_in
    attn = axial_gqa_attention(u[..., OFF_Q:OFF_K], u[..., OFF_K:OFF_V], u[..., OFF_V:OFF_R], q_gain, k_gain)
    z = centred_shift_mix(u[..., OFF_R:OFF_GATE].astype(jnp.float32), shift_mu.astype(jnp.float32))
    rw = rwkv7_bidirectional(z, w0, w_up, a0, a_up, g_up, k_k, k_a, r_k, lnx_w, lnx_b).astype(dt)
    gates = jax.nn.sigmoid(u[..., OFF_GATE:IN_COLS].astype(jnp.float32)).reshape(b, n, N_BRANCH, D_MODEL)
    br = jnp.stack([attn.astype(dt), rw], axis=2)
    br = jnp.einsum('btjc,jcd->btjd', br, w_branch).astype(jnp.float32)
    merged = jnp.sum(gates * br, axis=2).astype(dt)
    x = x + rms_norm(merged @ w_out, ln_mix_post)
    h = rms_norm(x, ln_ffn_pre)
    gu = h @ w_ffn_in
    f = (jax.nn.silu(gu[..., :D_FF]) * gu[..., D_FF:]) @ w_ffn_out
    return x + rms_norm(f, ln_ffn_post)


def setup_inputs(seed: int = 0) -> dict:
    key = jax.random.key(seed)
    ks = jax.random.split(key, 32)
    nrm = lambda k, shape, s: jax.random.normal(k, shape, jnp.float32) * s
    gain = lambda k, shape: 1.0 + 0.05 * jax.random.normal(k, shape, jnp.float32)
    L = DEPTH
    return {
        "x_prompt": nrm(ks[0], (BATCH, SEQ, D_MODEL), 1.0),
        "x_sample": nrm(ks[1], (DEC_BATCH, DEC_SEQ, D_MODEL), 1.0),
        "ln_mix_pre": gain(ks[2], (L, D_MODEL)),
        "ln_mix_post": gain(ks[3], (L, D_MODEL)),
        "ln_ffn_pre": gain(ks[4], (L, D_MODEL)),
        "ln_ffn_post": gain(ks[5], (L, D_MODEL)),
        "w_in": nrm(ks[6], (L, D_MODEL, IN_COLS), D_MODEL ** -0.5),
        "q_gain": gain(ks[7], (L, HEAD_DIM)),
        "k_gain": gain(ks[8], (L, HEAD_DIM)),
        "shift_mu": jax.random.uniform(ks[9], (L, SHIFT_COLS), jnp.float32),
        "w0": jax.random.uniform(ks[10], (L, N_DIR, RWKV_WIDTH), jnp.float32, -6.0, 0.0),
        "w_up": nrm(ks[11], (L, N_DIR, DECAY_LORA, RWKV_WIDTH), 0.5 * DECAY_LORA ** -0.5),
        "a0": nrm(ks[12], (L, N_DIR, RWKV_WIDTH), 0.2),
        "a_up": nrm(ks[13], (L, N_DIR, AAA_LORA, RWKV_WIDTH), 0.2 * AAA_LORA ** -0.5),
        "g_up": nrm(ks[14], (L, GATE_LORA, RWKV_WIDTH), GATE_LORA ** -0.5),
        "k_k": 0.85 + nrm(ks[15], (L, RWKV_WIDTH), 0.05),
        "k_a": gain(ks[16], (L, RWKV_WIDTH)),
        "r_k": nrm(ks[17], (L, RWKV_HEADS, RWKV_HEAD), 0.1),
        "lnx_w": gain(ks[18], (L, RWKV_WIDTH)),
        "lnx_b": nrm(ks[19], (L, RWKV_WIDTH), 0.01),
        "w_branch": nrm(ks[20], (L, N_BRANCH, ATTN_WIDTH, D_MODEL), ATTN_WIDTH ** -0.5),
        "w_out": nrm(ks[21], (L, D_MODEL, D_MODEL), D_MODEL ** -0.5),
        "w_ffn_in": nrm(ks[22], (L, D_MODEL, 2 * D_FF), D_MODEL ** -0.5),
        "w_ffn_out": nrm(ks[23], (L, D_FF, D_MODEL), D_FF ** -0.5),
    }


def reference(x_prompt, x_sample, ln_mix_pre, ln_mix_post, ln_ffn_pre, ln_ffn_post, w_in, q_gain, k_gain,
              shift_mu, w0, w_up, a0, a_up, g_up, k_k, k_a, r_k, lnx_w, lnx_b, w_branch, w_out,
              w_ffn_in, w_ffn_out):
    def trunk(x):
        for l in range(DEPTH):
            x = encoder_layer(x, ln_mix_pre[l], ln_mix_post[l], ln_ffn_pre[l], ln_ffn_post[l], w_in[l],
                              q_gain[l], k_gain[l], shift_mu[l], w0[l], w_up[l], a0[l], a_up[l], g_up[l],
                              k_k[l], k_a[l], r_k[l], lnx_w[l], lnx_b[l], w_branch[l], w_out[l],
                              w_ffn_in[l], w_ffn_out[l])
        return x

    y_prompt = trunk(x_prompt)
    y_sample = trunk(x_sample)
    return (y_prompt, y_sample)
```

```python
import functools
import math

import jax
import jax.numpy as jnp
from jax import lax
from jax.experimental import pallas as pl
from jax.experimental.pallas import tpu as pltpu

F32 = jnp.float32
BF16 = jnp.bfloat16

D_MODEL = 1024
GRID_W = 64
HEAD_DIM = 64
ATTN_HEADS = 8
ATTN_KV_HEADS = 2
ATTN_GROUP = ATTN_HEADS // ATTN_KV_HEADS
ATTN_WIDTH = ATTN_HEADS * HEAD_DIM
KV_WIDTH = ATTN_KV_HEADS * HEAD_DIM
ROPE_THETA = 10000.0
RWKV_HEADS = 8
RWKV_HEAD = 64
RWKV_WIDTH = RWKV_HEADS * RWKV_HEAD
DECAY_LORA = 64
AAA_LORA = 64
GATE_LORA = 128
LNX_EPS = 64e-5
D_FF = 2816
NORM_EPS = 1e-6

OFF_Q = 0
OFF_K = OFF_Q + ATTN_WIDTH
OFF_V = OFF_K + KV_WIDTH
OFF_R = OFF_V + KV_WIDTH
OFF_WD = OFF_R + 3 * RWKV_WIDTH
OFF_AD = OFF_WD + 2 * DECAY_LORA
OFF_GD = OFF_AD + 2 * AAA_LORA
OFF_GATE = OFF_GD + GATE_LORA
IN_COLS = OFF_GATE + 2 * D_MODEL
SHIFT_COLS = OFF_GATE - OFF_R

LANES = 128
VMEM_LIMIT = 56 * 1024 * 1024

TM_IN = 512
TQ = 256
TK = 512
TM_PREP = 256
CHUNK = 64
TM_MERGE = 512
TM_FFN = 512
TF_FFN = 256

_SCAN_PREC = lax.Precision.HIGHEST


def _cparams(sem):
    return pltpu.CompilerParams(dimension_semantics=sem, vmem_limit_bytes=VMEM_LIMIT)


def _sigmoid(x):
    return 1.0 / (1.0 + jnp.exp(-x))


def _full(shape):
    nd = len(shape)
    return pl.BlockSpec(shape, lambda *_: (0,) * nd)


def _seg_dot2(x, seg):
    hi = x.astype(BF16)
    lo = (x - hi.astype(F32)).astype(BF16)
    return (jnp.dot(hi, seg, preferred_element_type=F32) + jnp.dot(lo, seg, preferred_element_type=F32))


def _in_proj_kernel(x_ref, g_ref, w_ref, qg_ref, kg_ref, cos_ref, sin_ref, seg_ref,
                    q_ref, k_ref, v_ref, p_ref, gate_ref):
    x = x_ref[...]
    ms = jnp.mean(x * x, axis=-1, keepdims=True)
    h = (x * lax.rsqrt(ms + NORM_EPS) * g_ref[...]).astype(BF16)
    cos = cos_ref[...]
    sin = sin_ref[...]
    tm = x.shape[0]
    even = (lax.broadcasted_iota(jnp.int32, (tm, LANES), 1) % 2) == 0

    def head_norm_rope(t, gain, seg):
        ss = jnp.dot((t * t).astype(BF16), seg, preferred_element_type=F32) * (1.0 / HEAD_DIM)
        tn = t * lax.rsqrt(ss + NORM_EPS) * gain
        outs = []
        for j in range(t.shape[1] // LANES):
            tj = tn[:, j * LANES:(j + 1) * LANES]
            partner = jnp.where(even, pltpu.roll(tj, LANES - 1, axis=1), pltpu.roll(tj, 1, axis=1))
            outs.append(tj * cos + partner * sin)
        return outs

    q = jnp.dot(h, w_ref[:, OFF_Q:OFF_K], preferred_element_type=F32)
    qr = head_norm_rope(q, qg_ref[...], seg_ref[...])
    scale = HEAD_DIM ** -0.5
    for j, qj in enumerate(qr):
        qj = (qj * scale).astype(BF16)
        q_ref[2 * j] = qj[:, :HEAD_DIM]
        q_ref[2 * j + 1] = qj[:, HEAD_DIM:]

    kv = jnp.dot(h, w_ref[:, OFF_K:OFF_R], preferred_element_type=F32)
    kr = head_norm_rope(kv[:, :KV_WIDTH], kg_ref[...], seg_ref[:KV_WIDTH, :KV_WIDTH])[0].astype(BF16)
    k_ref[0] = kr[:, :HEAD_DIM]
    k_ref[1] = kr[:, HEAD_DIM:]
    vv = kv[:, KV_WIDTH:].astype(BF16)
    ones = jnp.ones((tm, HEAD_DIM), BF16)
    v_ref[0] = jnp.concatenate([vv[:, :HEAD_DIM], ones], axis=1)
    v_ref[1] = jnp.concatenate([vv[:, HEAD_DIM:], ones], axis=1)

    step = 640
    for c in range(SHIFT_COLS // step):
        p_ref[:, c * step:(c + 1) * step] = jnp.dot(
            h, w_ref[:, OFF_R + c * step:OFF_R + (c + 1) * step], preferred_element_type=F32)
    step = 512
    for c in range(2 * D_MODEL // step):
        gt = jnp.dot(h, w_ref[:, OFF_GATE + c * step:OFF_GATE + (c + 1) * step], preferred_element_type=F32)
        gate_ref[:, c * step:(c + 1) * step] = _sigmoid(gt).astype(BF16)


def _in_proj(x, g, w_bf, qg, kg, cos, sin, seg, seq_len):
    n = x.shape[0]
    tm = TM_IN
    tiles_per_seq = seq_len // tm
    tok = lambda i: (i, 0)
    return pl.pallas_call(
        _in_proj_kernel,
        grid=(n // tm,),
        in_specs=[
            pl.BlockSpec((tm, D_MODEL), tok),
            _full((1, D_MODEL)),
            _full((D_MODEL, IN_COLS)),
            _full((1, ATTN_WIDTH)),
            _full((1, KV_WIDTH)),
            pl.BlockSpec((tm, LANES), lambda i: (i % tiles_per_seq, 0)),
            pl.BlockSpec((tm, LANES), lambda i: (i % tiles_per_seq, 0)),
            _full((ATTN_WIDTH, ATTN_WIDTH)),
        ],
        out_specs=[
            pl.BlockSpec((ATTN_HEADS, tm, HEAD_DIM), lambda i: (0, i, 0)),
            pl.BlockSpec((ATTN_KV_HEADS, tm, HEAD_DIM), lambda i: (0, i, 0)),
            pl.BlockSpec((ATTN_KV_HEADS, tm, LANES), lambda i: (0, i, 0)),
            pl.BlockSpec((tm, SHIFT_COLS), tok),
            pl.BlockSpec((tm, 2 * D_MODEL), tok),
        ],
        out_shape=[
            jax.ShapeDtypeStruct((ATTN_HEADS, n, HEAD_DIM), BF16),
            jax.ShapeDtypeStruct((ATTN_KV_HEADS, n, HEAD_DIM), BF16),
            jax.ShapeDtypeStruct((ATTN_KV_HEADS, n, LANES), BF16),
            jax.ShapeDtypeStruct((n, SHIFT_COLS), F32),
            jax.ShapeDtypeStruct((n, 2 * D_MODEL), BF16),
        ],
        compiler_params=_cparams(("parallel",)),
        name="in_proj",
    )(x, g, w_bf, qg, kg, cos, sin, seg)


def _attn_kernel(q_ref, k_ref, v_ref, o_ref, m_sc, acc_sc):
    ki = pl.program_id(3)

    @pl.when(ki == 0)
    def _():
        m_sc[...] = jnp.full(m_sc.shape, -jnp.inf, F32)
        acc_sc[...] = jnp.zeros(acc_sc.shape, F32)

    g, tq, hd = q_ref.shape
    q = q_ref[...].reshape(g * tq, hd)
    s = lax.dot_general(q, k_ref[...], (((1,), (1,)), ((), ())), preferred_element_type=F32)
    m_prev = m_sc[...]
    m_next = jnp.maximum(m_prev, jnp.max(s, axis=1, keepdims=True))
    alpha = jnp.exp(m_prev - m_next)
    p = jnp.exp(s - pltpu.repeat(m_next, s.shape[1] // LANES, axis=1))
    acc_sc[...] = acc_sc[...] * alpha + jnp.dot(p.astype(BF16), v_ref[...], preferred_element_type=F32)
    m_sc[...] = m_next

    @pl.when(ki == pl.num_programs(3) - 1)
    def _():
        acc = acc_sc[...]
        o = acc / pltpu.roll(acc, HEAD_DIM, axis=1)
        o_ref[...] = jnp.concatenate([o[i * tq:(i + 1) * tq, :hd] for i in range(g)], axis=1).astype(o_ref.dtype)


def _attention(q, k, v, batch, seq_len):
    n = q.shape[1]
    nq = seq_len // TQ
    nk = seq_len // TK
    rows = ATTN_GROUP * TQ
    return pl.pallas_call(
        _attn_kernel,
        grid=(batch, ATTN_KV_HEADS, nq, nk),
        in_specs=[
            pl.BlockSpec((ATTN_GROUP, TQ, HEAD_DIM), lambda b, h, qi, ki: (h, b * nq + qi, 0)),
            pl.BlockSpec((None, TK, HEAD_DIM), lambda b, h, qi, ki: (h, b * nk + ki, 0)),
            pl.BlockSpec((None, TK, LANES), lambda b, h, qi, ki: (h, b * nk + ki, 0)),
        ],
        out_specs=pl.BlockSpec((TQ, ATTN_GROUP * HEAD_DIM), lambda b, h, qi, ki: (b * nq + qi, h)),
        out_shape=jax.ShapeDtypeStruct((n, ATTN_WIDTH), BF16),
        scratch_shapes=[pltpu.VMEM((rows, LANES), F32), pltpu.VMEM((rows, LANES), F32)],
        compiler_params=_cparams(("parallel", "parallel", "parallel", "arbitrary")),
        name="attention",
    )(q, k, v)


def _prep_kernel(p_ref, pprev_ref, pnext_ref, mu_ref, wup_ref, aup_ref, gup_ref, w0_ref, a0_ref,
                 kkw_ref, ka_ref, rk_ref, seg_ref,
                 r_o, v_o, kk_o, ld0_o, ld1_o, kd0_o, kd1_o, b0_o, b1_o, bonus_o, g_o, *, tiles_per_seq):
    pos = pl.program_id(0) % tiles_per_seq
    p = p_ref[...]
    tm = p.shape[0]
    row = lax.broadcasted_iota(jnp.int32, (tm, 1), 0)
    prev_row = jnp.where(pos == 0, 0.0, pprev_ref[7:8, :])
    next_row = jnp.where(pos == tiles_per_seq - 1, 0.0, pnext_ref[0:1, :])
    prev = jnp.where(row == 0, prev_row, pltpu.roll(p, 1, axis=0))
    nxt = jnp.where(row == tm - 1, next_row, pltpu.roll(p, tm - 1, axis=0))
    z = p + (0.5 * (prev + nxt) - p) * mu_ref[...]

    w = RWKV_WIDTH
    r = z[:, 0:w]
    k = z[:, w:2 * w]
    v = z[:, 2 * w:3 * w]
    wd = z[:, 3 * w:3 * w + 2 * DECAY_LORA]
    ad = z[:, 3 * w + 2 * DECAY_LORA:3 * w + 2 * DECAY_LORA + 2 * AAA_LORA]
    gd = z[:, 3 * w + 2 * DECAY_LORA + 2 * AAA_LORA:]

    wpre = w0_ref[...] + jnp.dot(jnp.tanh(wd).astype(BF16), wup_ref[...], preferred_element_type=F32)
    ld = -math.exp(-0.5) * _sigmoid(wpre)
    a = _sigmoid(a0_ref[...] + jnp.dot(ad.astype(BF16), aup_ref[...], preferred_element_type=F32))
    g = jnp.dot(_sigmoid(gd).astype(BF16), gup_ref[...], preferred_element_type=F32)

    seg = seg_ref[...]
    kk = k * kkw_ref[...]
    kk = kk / jnp.maximum(jnp.sqrt(_seg_dot2(kk * kk, seg)), 1e-12)
    ka = ka_ref[...]
    kd0 = k * (1.0 + (a[:, :w] - 1.0) * ka)
    kd1 = k * (1.0 + (a[:, w:] - 1.0) * ka)
    bonus = _seg_dot2(r * (0.5 * (kd0 + kd1)) * rk_ref[...], seg) * v

    r_o[...] = r
    v_o[...] = v
    kk_o[...] = kk
    ld0_o[...] = ld[:, :w]
    ld1_o[...] = ld[:, w:]
    kd0_o[...] = kd0
    kd1_o[...] = kd1
    b0_o[...] = kk * a[:, :w]
    b1_o[...] = kk * a[:, w:]
    bonus_o[...] = bonus
    g_o[...] = g


def _rwkv_prep(p, mu, wup, aup, gup, w0, a0, kkw, ka, rk, seg, seq_len):
    n = p.shape[0]
    tm = TM_PREP
    tiles_per_seq = seq_len // tm
    rows8 = tm // 8
    last8 = n // 8 - 1
    tok = lambda i: (i, 0)
    w = RWKV_WIDTH
    out_spec = pl.BlockSpec((tm, w), tok)
    return pl.pallas_call(
        functools.partial(_prep_kernel, tiles_per_seq=tiles_per_seq),
        grid=(n // tm,),
        in_specs=[
            pl.BlockSpec((tm, SHIFT_COLS), tok),
            pl.BlockSpec((8, SHIFT_COLS), lambda i: (jnp.maximum(i * rows8 - 1, 0), 0)),
            pl.BlockSpec((8, SHIFT_COLS), lambda i: (jnp.minimum((i + 1) * rows8, last8), 0)),
            _full((1, SHIFT_COLS)),
            _full((2 * DECAY_LORA, 2 * w)),
            _full((2 * AAA_LORA, 2 * w)),
            _full((GATE_LORA, w)),
            _full((1, 2 * w)),
            _full((1, 2 * w)),
            _full((1, w)),
            _full((1, w)),
            _full((1, w)),
            _full((w, w)),
        ],
        out_specs=[out_spec] * 11,
        out_shape=[jax.ShapeDtypeStruct((n, w), F32)] * 11,
        compiler_params=_cparams(("parallel",)),
        name="rwkv_prep",
    )(p, p, p, mu, wup, aup, gup, w0, a0, kkw, ka, rk, seg)


def _mm(a, b):
    return jnp.dot(a, b, preferred_element_type=F32, precision=_SCAN_PREC)


def _mm_nt(a, b):
    return lax.dot_general(a, b, (((1,), (1,)), ((), ())), preferred_element_type=F32, precision=_SCAN_PREC)


def _mm_tn(a, b):
    return lax.dot_general(a, b, (((0,), (0,)), ((), ())), preferred_element_type=F32, precision=_SCAN_PREC)


def _cumsum_rows(tri, x):
    hi = x.astype(BF16)
    r1 = x - hi.astype(F32)
    mid = r1.astype(BF16)
    lo = (r1 - mid.astype(F32)).astype(BF16)
    d = lambda y: jnp.dot(tri, y, preferred_element_type=F32)
    return d(hi) + d(mid) + d(lo)


def _scan_chunk(r, v, kk, ld, kd, b, s_ref, y_ref, reverse):
    c = r.shape[0]
    ri = lax.broadcasted_iota(jnp.int32, (c, c), 0)
    ci = lax.broadcasted_iota(jnp.int32, (c, c), 1)
    tri = ((ci >= ri) if reverse else (ci <= ri)).astype(BF16)
    gam = _cumsum_rows(tri, ld)
    gprev = gam - ld
    g_end = gam[0:1, :] if reverse else gam[c - 1:c, :]
    ginv = jnp.exp(-gam)
    a_t = -kk * jnp.exp(gprev)
    b_t = b * ginv
    k_t = kd * ginv
    r_t = r * jnp.exp(gam)
    dec = jnp.exp(g_end - gam)
    b_h = b * dec
    k_h = kd * dec
    g_c = jnp.exp(g_end)

    lane = lax.broadcasted_iota(jnp.int32, (c, LANES), 1)
    row = lax.broadcasted_iota(jnp.int32, (c, LANES), 0)
    lo_half = lane < RWKV_HEAD
    s_idx = lane % c
    strict = (s_idx > row) if reverse else (s_idx < row)
    incl = (s_idx >= row) if reverse else (s_idx <= row)
    eye2 = (s_idx == row).astype(F32)
    bd_r = lax.broadcasted_iota(jnp.int32, (LANES, LANES), 0) // RWKV_HEAD
    bd_c = lax.broadcasted_iota(jnp.int32, (LANES, LANES), 1) // RWKV_HEAD
    blockdiag = bd_r == bd_c

    def sm(x):
        return jnp.concatenate([jnp.where(lo_half, x, 0.0), jnp.where(lo_half, 0.0, x)], axis=0)

    for pr in range(RWKV_WIDTH // LANES):
        sl = slice(pr * LANES, (pr + 1) * LANES)
        at, bt, kt, rt, vp = a_t[:, sl], b_t[:, sl], k_t[:, sl], r_t[:, sl], v[:, sl]
        aa = _mm_nt(jnp.concatenate([at, rt], axis=0), jnp.concatenate([sm(bt), sm(kt)], axis=0))
        a_ab = jnp.where(strict, aa[:c, :LANES], 0.0)
        a_ak = jnp.where(strict, aa[:c, LANES:], 0.0)
        a_rb = jnp.where(incl, aa[c:, :LANES], 0.0)
        a_rk = jnp.where(incl, aa[c:, LANES:], 0.0)
        pw = a_ab
        t_inv = eye2 + pw
        for _ in range(int(math.log2(c)) - 1):
            pw = _mm(pw, sm(pw))
            t_inv = t_inv + _mm(t_inv, sm(pw))
        av = _mm(a_ak, sm(vp))
        wx = _mm(t_inv, jnp.concatenate([sm(at), sm(av)], axis=1))
        s0 = s_ref[pr]
        us = _mm_nt(jnp.concatenate([wx[:, :LANES], rt], axis=0), s0)
        u = us[:c] + wx[:, LANES:]
        y = us[c:] + _mm(jnp.concatenate([a_rb, a_rk], axis=1), jnp.concatenate([sm(u), sm(vp)], axis=0))
        y_ref[:, sl] = y
        upd = _mm_tn(jnp.concatenate([u, vp], axis=0), jnp.concatenate([b_h[:, sl], k_h[:, sl]], axis=0))
        s_ref[pr] = s0 * g_c[:, sl] + jnp.where(blockdiag, upd, 0.0)


def _scan_kernel(rf, vf, kkf, ldf, kdf, bf, rb, vb, kkb, ldb, kdb, bb, yf_ref, yb_ref, s_sc):
    @pl.when(pl.program_id(1) == 0)
    def _():
        s_sc[...] = jnp.zeros(s_sc.shape, F32)

    _scan_chunk(rf[...], vf[...], kkf[...], ldf[...], kdf[...], bf[...], s_sc.at[0], yf_ref, False)
    _scan_chunk(rb[...], vb[...], kkb[...], ldb[...], kdb[...], bb[...], s_sc.at[1], yb_ref, True)


def _rwkv_scan(r, v, kk, ld0, ld1, kd0, kd1, b0, b1, batch, seq_len):
    n = r.shape[0]
    c = CHUNK
    nc = seq_len // c
    w = RWKV_WIDTH
    fwd = pl.BlockSpec((c, w), lambda b, i: (b * nc + i, 0))
    bwd = pl.BlockSpec((c, w), lambda b, i: (b * nc + nc - 1 - i, 0))
    return pl.pallas_call(
        _scan_kernel,
        grid=(batch, nc),
        in_specs=[fwd] * 6 + [bwd] * 6,
        out_specs=[fwd, bwd],
        out_shape=[jax.ShapeDtypeStruct((n, w), F32)] * 2,
        scratch_shapes=[pltpu.VMEM((2, w // LANES, LANES, LANES), F32)],
        compiler_params=_cparams(("parallel", "arbitrary")),
        name="rwkv_scan",
    )(r, v, kk, ld0, kd0, b0, r, v, kk, ld1, kd1, b1)


def _merge_kernel(x_ref, attn_ref, yf_ref, yb_ref, bonus_ref, g_ref, gate_ref, lnxw_ref, lnxb_ref,
                  seg_ref, wb0_ref, wb1_ref, wout_ref, gpost_ref, o_ref):
    seg = seg_ref[...]
    y = yf_ref[...] + yb_ref[...]
    mu = _seg_dot2(y, seg) * (1.0 / RWKV_HEAD)
    yc = y - mu
    var = _seg_dot2(yc * yc, seg) * (1.0 / RWKV_HEAD)
    yn = yc * lax.rsqrt(var + LNX_EPS) * lnxw_ref[...] + lnxb_ref[...]
    rw = ((yn + bonus_ref[...]) * g_ref[...]).astype(BF16)
    br0 = jnp.dot(attn_ref[...], wb0_ref[...], preferred_element_type=F32)
    br1 = jnp.dot(rw, wb1_ref[...], preferred_element_type=F32)
    gates = gate_ref[...]
    merged = gates[:, :D_MODEL].astype(F32) * br0 + gates[:, D_MODEL:].astype(F32) * br1
    mo = jnp.dot(merged.astype(BF16), wout_ref[...], preferred_element_type=F32)
    ms = jnp.mean(mo * mo, axis=-1, keepdims=True)
    o_ref[...] = x_ref[...] + mo * lax.rsqrt(ms + NORM_EPS) * gpost_ref[...]


def _merge(x, attn, yf, yb, bonus, g, gates, lnxw, lnxb, seg, wb0, wb1, wout, gpost):
    n = x.shape[0]
    tm = TM_MERGE
    w = RWKV_WIDTH
    tok = lambda i: (i, 0)
    return pl.pallas_call(
        _merge_kernel,
        grid=(n // tm,),
        in_specs=[
            pl.BlockSpec((tm, D_MODEL), tok),
            pl.BlockSpec((tm, ATTN_WIDTH), tok),
            pl.BlockSpec((tm, w), tok),
            pl.BlockSpec((tm, w), tok),
            pl.BlockSpec((tm, w), tok),
            pl.BlockSpec((tm, w), tok),
            pl.BlockSpec((tm, 2 * D_MODEL), tok),
            _full((1, w)),
            _full((1, w)),
            _full((w, w)),
            _full((ATTN_WIDTH, D_MODEL)),
            _full((w, D_MODEL)),
            _full((D_MODEL, D_MODEL)),
            _full((1, D_MODEL)),
        ],
        out_specs=pl.BlockSpec((tm, D_MODEL), tok),
        out_shape=jax.ShapeDtypeStruct((n, D_MODEL), F32),
        compiler_params=_cparams(("parallel",)),
        name="merge",
    )(x, attn, yf, yb, bonus, g, gates, lnxw, lnxb, seg, wb0, wb1, wout, gpost)


def _ffn_kernel(x_ref, gpre_ref, wg_ref, wu_ref, wo_ref, gpost_ref, o_ref, h_sc, acc_sc):
    j = pl.program_id(1)

    @pl.when(j == 0)
    def _():
        x = x_ref[...]
        ms = jnp.mean(x * x, axis=-1, keepdims=True)
        h_sc[...] = (x * lax.rsqrt(ms + NORM_EPS) * gpre_ref[...]).astype(BF16)
        acc_sc[...] = jnp.zeros(acc_sc.shape, F32)

    h = h_sc[...]
    gg = jnp.dot(h, wg_ref[...], preferred_element_type=F32)
    uu = jnp.dot(h, wu_ref[...], preferred_element_type=F32)
    f = (gg * _sigmoid(gg) * uu).astype(BF16)
    acc_sc[...] += jnp.dot(f, wo_ref[...], preferred_element_type=F32)

    @pl.when(j == pl.num_programs(1) - 1)
    def _():
        acc = acc_sc[...]
        ms = jnp.mean(acc * acc, axis=-1, keepdims=True)
        o_ref[...] = x_ref[...] + acc * lax.rsqrt(ms + NORM_EPS) * gpost_ref[...]


def _ffn(x, gpre, w_in_bf, w_out_bf, gpost):
    n = x.shape[0]
    tm, tf = TM_FFN, TF_FFN
    nf = D_FF // tf
    return pl.pallas_call(
        _ffn_kernel,
        grid=(n // tm, nf),
        in_specs=[
            pl.BlockSpec((tm, D_MODEL), lambda i, j: (i, 0)),
            _full((1, D_MODEL)),
            pl.BlockSpec((D_MODEL, tf), lambda i, j: (0, j)),
            pl.BlockSpec((D_MODEL, tf), lambda i, j: (0, nf + j)),
            pl.BlockSpec((tf, D_MODEL), lambda i, j: (j, 0)),
            _full((1, D_MODEL)),
        ],
        out_specs=pl.BlockSpec((tm, D_MODEL), lambda i, j: (i, 0)),
        out_shape=jax.ShapeDtypeStruct((n, D_MODEL), F32),
        scratch_shapes=[pltpu.VMEM((tm, D_MODEL), BF16), pltpu.VMEM((tm, D_MODEL), F32)],
        compiler_params=_cparams(("parallel", "arbitrary")),
        name="ffn",
    )(x, gpre, w_in_bf, w_in_bf, w_out_bf, gpost)


def _rope_tables(n):
    rows = n // GRID_W
    row = jnp.broadcast_to(jnp.arange(rows)[:, None], (rows, GRID_W)).reshape(-1).astype(F32)
    col = jnp.broadcast_to(jnp.arange(GRID_W)[None, :], (rows, GRID_W)).reshape(-1).astype(F32)
    nf = HEAD_DIM // 4
    inv = ROPE_THETA ** (-jnp.arange(nf, dtype=F32) / nf)
    ang = jnp.concatenate([row[:, None] * inv, col[:, None] * inv], axis=-1)
    cos = jnp.repeat(jnp.cos(ang), 2, axis=-1)
    sin = jnp.repeat(jnp.sin(ang), 2, axis=-1)
    sign = jnp.tile(jnp.array([-1.0, 1.0], F32), HEAD_DIM // 2)
    reps = LANES // HEAD_DIM
    return jnp.tile(cos, (1, reps)), jnp.tile(sin * sign, (1, reps))


def _block_diag2(m):
    z = jnp.zeros_like(m[0])
    return jnp.concatenate([jnp.concatenate([m[0], z], axis=1), jnp.concatenate([z, m[1]], axis=1)], axis=0)


def _trunk(x3, prm):
    batch, seq_len, _ = x3.shape
    n = batch * seq_len
    x = x3.reshape(n, D_MODEL)
    cos, sin = _rope_tables(seq_len)
    q, k, v, p, gates = _in_proj(x, prm["ln_mix_pre"], prm["w_in"], prm["q_gain"], prm["k_gain"],
                                 cos, sin, prm["seg"], seq_len)
    attn = _attention(q, k, v, batch, seq_len)
    r, vv, kk, ld0, ld1, kd0, kd1, b0, b1, bonus, g = _rwkv_prep(
        p, prm["shift_mu"], prm["w_up"], prm["a_up"], prm["g_up"], prm["w0"], prm["a0"],
        prm["k_k"], prm["k_a"], prm["r_k"], prm["seg"], seq_len)
    yf, yb = _rwkv_scan(r, vv, kk, ld0, ld1, kd0, kd1, b0, b1, batch, seq_len)
    x1 = _merge(x, attn, yf, yb, bonus, g, gates, prm["lnx_w"], prm["lnx_b"], prm["seg"],
                prm["wb0"], prm["wb1"], prm["w_out"], prm["ln_mix_post"])
    x2 = _ffn(x1, prm["ln_ffn_pre"], prm["w_ffn_in"], prm["w_ffn_out"], prm["ln_ffn_post"])
    return x2.reshape(batch, seq_len, D_MODEL)


def _prepare_params(ln_mix_pre, ln_mix_post, ln_ffn_pre, ln_ffn_post, w_in, q_gain, k_gain, shift_mu,
                    w0, w_up, a0, a_up, g_up, k_k, k_a, r_k, lnx_w, lnx_b, w_branch, w_out, w_ffn_in, w_ffn_out):
    row = lambda t: t.reshape(1, -1).astype(F32)
    head = jnp.arange(RWKV_WIDTH) // RWKV_HEAD
    seg = (head[:, None] == head[None, :]).astype(BF16)
    return {
        "ln_mix_pre": row(ln_mix_pre), "ln_mix_post": row(ln_mix_post),
        "ln_ffn_pre": row(ln_ffn_pre), "ln_ffn_post": row(ln_ffn_post),
        "w_in": w_in.astype(BF16),
        "q_gain": row(jnp.tile(q_gain, ATTN_HEADS)), "k_gain": row(jnp.tile(k_gain, ATTN_KV_HEADS)),
        "shift_mu": row(shift_mu),
        "w0": row(w0), "a0": row(a0),
        "w_up": _block_diag2(w_up).astype(BF16), "a_up": _block_diag2(a_up).astype(BF16),
        "g_up": g_up.astype(BF16),
        "k_k": row(k_k), "k_a": row(k_a), "r_k": row(r_k),
        "lnx_w": row(lnx_w), "lnx_b": row(lnx_b),
        "wb0": w_branch[0].astype(BF16), "wb1": w_branch[1].astype(BF16),
        "w_out": w_out.astype(BF16),
        "w_ffn_in": w_ffn_in.astype(BF16), "w_ffn_out": w_ffn_out.astype(BF16),
        "seg": seg,
    }


def kernel(x_prompt, x_sample, ln_mix_pre, ln_mix_post, ln_ffn_pre, ln_ffn_post, w_in, q_gain, k_gain, shift_mu,
           w0, w_up, a0, a_up, g_up, k_k, k_a, r_k, lnx_w, lnx_b, w_branch, w_out, w_ffn_in, w_ffn_out):
    weights = (ln_mix_pre, ln_mix_post, ln_ffn_pre, ln_ffn_post, w_in, q_gain, k_gain, shift_mu,
               w0, w_up, a0, a_up, g_up, k_k, k_a, r_k, lnx_w, lnx_b, w_branch, w_out, w_ffn_in, w_ffn_out)
    assert all(t.shape[0] == 1 for t in weights), "single-layer stack expected"
    prm = _prepare_params(*(t[0] for t in weights))
    return (_trunk(x_prompt, prm), _trunk(x_sample, prm))
```

```python
import functools
import math

import jax
import jax.numpy as jnp
from jax import lax
from jax.experimental import pallas as pl
from jax.experimental.pallas import tpu as pltpu

F32 = jnp.float32
BF16 = jnp.bfloat16

D_MODEL = 1024
GRID_W = 64
HEAD_DIM = 64
ATTN_HEADS = 8
ATTN_KV_HEADS = 2
ATTN_GROUP = ATTN_HEADS // ATTN_KV_HEADS
ATTN_WIDTH = ATTN_HEADS * HEAD_DIM
KV_WIDTH = ATTN_KV_HEADS * HEAD_DIM
ROPE_THETA = 10000.0
RWKV_HEADS = 8
RWKV_HEAD = 64
RWKV_WIDTH = RWKV_HEADS * RWKV_HEAD
DECAY_LORA = 64
AAA_LORA = 64
GATE_LORA = 128
LNX_EPS = 64e-5
D_FF = 2816
NORM_EPS = 1e-6

OFF_Q = 0
OFF_K = OFF_Q + ATTN_WIDTH
OFF_V = OFF_K + KV_WIDTH
OFF_R = OFF_V + KV_WIDTH
OFF_WD = OFF_R + 3 * RWKV_WIDTH
OFF_AD = OFF_WD + 2 * DECAY_LORA
OFF_GD = OFF_AD + 2 * AAA_LORA
OFF_GATE = OFF_GD + GATE_LORA
IN_COLS = OFF_GATE + 2 * D_MODEL
SHIFT_COLS = OFF_GATE - OFF_R

LANES = 128
VMEM_LIMIT = 56 * 1024 * 1024

TM_IN = 512
TQ = 256
TK = 1024
TM_PREP = 256
CHUNK = 64
TM_MERGE = 512
TM_FFN = 512
TF_FFN = D_FF // 2


def _cparams(sem):
    return pltpu.CompilerParams(dimension_semantics=sem, vmem_limit_bytes=VMEM_LIMIT)


def _sigmoid(x):
    return 1.0 / (1.0 + jnp.exp(-x))


def _full(shape):
    nd = len(shape)
    return pl.BlockSpec(shape, lambda *_: (0,) * nd)


def _seg_dot2(x, seg):
    hi = x.astype(BF16)
    lo = (x - hi.astype(F32)).astype(BF16)
    return (jnp.dot(hi, seg, preferred_element_type=F32) + jnp.dot(lo, seg, preferred_element_type=F32))


def _in_proj_kernel(x_ref, g_ref, w_ref, qg_ref, kg_ref, cos_ref, sin_ref, seg_ref,
                    q_ref, k_ref, v_ref, p_ref, gate_ref):
    x = x_ref[...]
    ms = jnp.mean(x * x, axis=-1, keepdims=True)
    h = (x * lax.rsqrt(ms + NORM_EPS) * g_ref[...]).astype(BF16)
    cos = cos_ref[...]
    sin = sin_ref[...]
    tm = x.shape[0]
    even = (lax.broadcasted_iota(jnp.int32, (tm, LANES), 1) % 2) == 0

    def head_norm_rope(t, gain, seg):
        ss = jnp.dot((t * t).astype(BF16), seg, preferred_element_type=F32) * (1.0 / HEAD_DIM)
        tn = t * lax.rsqrt(ss + NORM_EPS) * gain
        outs = []
        for j in range(t.shape[1] // LANES):
            tj = tn[:, j * LANES:(j + 1) * LANES]
            partner = jnp.where(even, pltpu.roll(tj, LANES - 1, axis=1), pltpu.roll(tj, 1, axis=1))
            outs.append(tj * cos + partner * sin)
        return outs

    q = jnp.dot(h, w_ref[:, OFF_Q:OFF_K], preferred_element_type=F32)
    qr = head_norm_rope(q, qg_ref[...], seg_ref[...])
    scale = HEAD_DIM ** -0.5 * math.log2(math.e)
    for j, qj in enumerate(qr):
        qj = (qj * scale).astype(BF16)
        q_ref[2 * j] = qj[:, :HEAD_DIM]
        q_ref[2 * j + 1] = qj[:, HEAD_DIM:]

    kv = jnp.dot(h, w_ref[:, OFF_K:OFF_R], preferred_element_type=F32)
    kr = head_norm_rope(kv[:, :KV_WIDTH], kg_ref[...], seg_ref[:KV_WIDTH, :KV_WIDTH])[0].astype(BF16)
    k_ref[0] = kr[:, :HEAD_DIM]
    k_ref[1] = kr[:, HEAD_DIM:]
    vt = kv[:, KV_WIDTH:].T
    ones = jnp.ones((HEAD_DIM, tm), F32)
    v_ref[0] = jnp.concatenate([vt[:HEAD_DIM], ones], axis=0).astype(BF16)
    v_ref[1] = jnp.concatenate([vt[HEAD_DIM:], ones], axis=0).astype(BF16)

    step = 640
    for c in range(SHIFT_COLS // step):
        p_ref[:, c * step:(c + 1) * step] = jnp.dot(
            h, w_ref[:, OFF_R + c * step:OFF_R + (c + 1) * step], preferred_element_type=F32)
    step = 512
    for c in range(2 * D_MODEL // step):
        gt = jnp.dot(h, w_ref[:, OFF_GATE + c * step:OFF_GATE + (c + 1) * step], preferred_element_type=F32)
        gate_ref[:, c * step:(c + 1) * step] = _sigmoid(gt).astype(BF16)


def _in_proj(x, g, w_bf, qg, kg, cos, sin, seg, seq_len):
    n = x.shape[0]
    tm = TM_IN
    tiles_per_seq = seq_len // tm
    tok = lambda i: (i, 0)
    return pl.pallas_call(
        _in_proj_kernel,
        grid=(n // tm,),
        in_specs=[
            pl.BlockSpec((tm, D_MODEL), tok),
            _full((1, D_MODEL)),
            _full((D_MODEL, IN_COLS)),
            _full((1, ATTN_WIDTH)),
            _full((1, KV_WIDTH)),
            pl.BlockSpec((tm, LANES), lambda i: (i % tiles_per_seq, 0)),
            pl.BlockSpec((tm, LANES), lambda i: (i % tiles_per_seq, 0)),
            _full((ATTN_WIDTH, ATTN_WIDTH)),
        ],
        out_specs=[
            pl.BlockSpec((ATTN_HEADS, tm, HEAD_DIM), lambda i: (0, i, 0)),
            pl.BlockSpec((ATTN_KV_HEADS, tm, HEAD_DIM), lambda i: (0, i, 0)),
            pl.BlockSpec((ATTN_KV_HEADS, LANES, tm), lambda i: (0, 0, i)),
            pl.BlockSpec((tm, SHIFT_COLS), tok),
            pl.BlockSpec((tm, 2 * D_MODEL), tok),
        ],
        out_shape=[
            jax.ShapeDtypeStruct((ATTN_HEADS, n, HEAD_DIM), BF16),
            jax.ShapeDtypeStruct((ATTN_KV_HEADS, n, HEAD_DIM), BF16),
            jax.ShapeDtypeStruct((ATTN_KV_HEADS, LANES, n), BF16),
            jax.ShapeDtypeStruct((n, SHIFT_COLS), F32),
            jax.ShapeDtypeStruct((n, 2 * D_MODEL), BF16),
        ],
        compiler_params=_cparams(("parallel",)),
        name="in_proj",
    )(x, g, w_bf, qg, kg, cos, sin, seg)


def _attn_kernel(q_ref, k_ref, vt_ref, o_ref, m_sc, acc_sc):
    ki = pl.program_id(3)

    @pl.when(ki == 0)
    def _():
        m_sc[...] = jnp.full(m_sc.shape, -jnp.inf, F32)
        acc_sc[...] = jnp.zeros(acc_sc.shape, F32)

    g, tq, hd = q_ref.shape
    k = k_ref[...]
    vt = vt_ref[...]
    scores = lambda i: lax.dot_general(k, q_ref[i], (((1,), (1,)), ((), ())), preferred_element_type=F32)

    def col_max(x):
        while x.shape[0] > 8:
            half = x.shape[0] // 2
            x = jnp.maximum(x[:half], x[half:])
        return jnp.max(x, axis=0, keepdims=True)

    st = scores(0)
    for i in range(g):
        st_next = scores(i + 1) if i + 1 < g else None
        cols = slice(i * tq, (i + 1) * tq)
        m_prev = m_sc[:, cols]
        m_next = jnp.maximum(m_prev, col_max(st))
        alpha = jnp.exp2(m_prev - m_next)
        p = jnp.exp2((st - m_next).astype(BF16))
        acc_sc[:, cols] = acc_sc[:, cols] * alpha + jnp.dot(vt, p, preferred_element_type=F32)
        m_sc[:, cols] = m_next
        st = st_next

    @pl.when(ki == pl.num_programs(3) - 1)
    def _():
        acc = acc_sc[...]
        o = (acc[:hd] / acc[hd:hd + 1]).astype(o_ref.dtype)
        for i in range(g):
            o_ref[i * hd:(i + 1) * hd, :] = o[:, i * tq:(i + 1) * tq]


def _attention(q, k, vt, batch, seq_len):
    n = q.shape[1]
    nq = seq_len // TQ
    nk = seq_len // TK
    cols = ATTN_GROUP * TQ
    return pl.pallas_call(
        _attn_kernel,
        grid=(batch, ATTN_KV_HEADS, nq, nk),
        in_specs=[
            pl.BlockSpec((ATTN_GROUP, TQ, HEAD_DIM), lambda b, h, qi, ki: (h, b * nq + qi, 0)),
            pl.BlockSpec((None, TK, HEAD_DIM), lambda b, h, qi, ki: (h, b * nk + ki, 0)),
            pl.BlockSpec((None, LANES, TK), lambda b, h, qi, ki: (h, 0, b * nk + ki)),
        ],
        out_specs=pl.BlockSpec((ATTN_GROUP * HEAD_DIM, TQ), lambda b, h, qi, ki: (h, b * nq + qi)),
        out_shape=jax.ShapeDtypeStruct((ATTN_WIDTH, n), BF16),
        scratch_shapes=[pltpu.VMEM((1, cols), F32), pltpu.VMEM((LANES, cols), F32)],
        compiler_params=_cparams(("parallel", "parallel", "parallel", "arbitrary")),
        name="attention",
    )(q, k, vt)


def _prep_kernel(p_ref, pprev_ref, pnext_ref, mu_ref, wup_ref, aup_ref, gup_ref, w0_ref, a0_ref,
                 kkw_ref, ka_ref, rk_ref, seg_ref,
                 r_o, v_o, kk_o, ld0_o, ld1_o, kd0_o, kd1_o, b0_o, b1_o, bonus_o, g_o, *, tiles_per_seq):
    pos = pl.program_id(0) % tiles_per_seq
    p = p_ref[...]
    tm = p.shape[0]
    row = lax.broadcasted_iota(jnp.int32, (tm, 1), 0)
    prev_row = jnp.where(pos == 0, 0.0, pprev_ref[7:8, :])
    next_row = jnp.where(pos == tiles_per_seq - 1, 0.0, pnext_ref[0:1, :])
    prev = jnp.where(row == 0, prev_row, pltpu.roll(p, 1, axis=0))
    nxt = jnp.where(row == tm - 1, next_row, pltpu.roll(p, tm - 1, axis=0))
    z = p + (0.5 * (prev + nxt) - p) * mu_ref[...]

    w = RWKV_WIDTH
    r = z[:, 0:w]
    k = z[:, w:2 * w]
    v = z[:, 2 * w:3 * w]
    wd = z[:, 3 * w:3 * w + 2 * DECAY_LORA]
    ad = z[:, 3 * w + 2 * DECAY_LORA:3 * w + 2 * DECAY_LORA + 2 * AAA_LORA]
    gd = z[:, 3 * w + 2 * DECAY_LORA + 2 * AAA_LORA:]

    wpre = w0_ref[...] + jnp.dot(jnp.tanh(wd).astype(BF16), wup_ref[...], preferred_element_type=F32)
    ld = -math.exp(-0.5) * _sigmoid(wpre)
    a = _sigmoid(a0_ref[...] + jnp.dot(ad.astype(BF16), aup_ref[...], preferred_element_type=F32))
    g = jnp.dot(_sigmoid(gd).astype(BF16), gup_ref[...], preferred_element_type=F32)

    seg = seg_ref[...]
    kk = k * kkw_ref[...]
    kk = kk / jnp.maximum(jnp.sqrt(_seg_dot2(kk * kk, seg)), 1e-12)
    ka = ka_ref[...]
    kd0 = k * (1.0 + (a[:, :w] - 1.0) * ka)
    kd1 = k * (1.0 + (a[:, w:] - 1.0) * ka)
    bonus = _seg_dot2(r * (0.5 * (kd0 + kd1)) * rk_ref[...], seg) * v

    r_o[...] = r
    v_o[...] = v
    kk_o[...] = kk
    ld0_o[...] = ld[:, :w]
    ld1_o[...] = ld[:, w:]
    kd0_o[...] = kd0
    kd1_o[...] = kd1
    b0_o[...] = kk * a[:, :w]
    b1_o[...] = kk * a[:, w:]
    bonus_o[...] = bonus
    g_o[...] = g


def _rwkv_prep(p, mu, wup, aup, gup, w0, a0, kkw, ka, rk, seg, seq_len):
    n = p.shape[0]
    tm = TM_PREP
    tiles_per_seq = seq_len // tm
    rows8 = tm // 8
    last8 = n // 8 - 1
    tok = lambda i: (i, 0)
    w = RWKV_WIDTH
    out_spec = pl.BlockSpec((tm, w), tok)
    return pl.pallas_call(
        functools.partial(_prep_kernel, tiles_per_seq=tiles_per_seq),
        grid=(n // tm,),
        in_specs=[
            pl.BlockSpec((tm, SHIFT_COLS), tok),
            pl.BlockSpec((8, SHIFT_COLS), lambda i: (jnp.maximum(i * rows8 - 1, 0), 0)),
            pl.BlockSpec((8, SHIFT_COLS), lambda i: (jnp.minimum((i + 1) * rows8, last8), 0)),
            _full((1, SHIFT_COLS)),
            _full((2 * DECAY_LORA, 2 * w)),
            _full((2 * AAA_LORA, 2 * w)),
            _full((GATE_LORA, w)),
            _full((1, 2 * w)),
            _full((1, 2 * w)),
            _full((1, w)),
            _full((1, w)),
            _full((1, w)),
            _full((w, w)),
        ],
        out_specs=[out_spec] * 11,
        out_shape=[jax.ShapeDtypeStruct((n, w), F32)] * 11,
        compiler_params=_cparams(("parallel",)),
        name="rwkv_prep",
    )(p, p, p, mu, wup, aup, gup, w0, a0, kkw, ka, rk, seg)


def _mm(a, b):
    return jnp.dot(a, b, preferred_element_type=F32)


def _mm_nt(a, b):
    return lax.dot_general(a, b, (((1,), (1,)), ((), ())), preferred_element_type=F32)


def _mm_tn(a, b):
    return lax.dot_general(a, b, (((0,), (0,)), ((), ())), preferred_element_type=F32)


def _cumsum_rows(tri, x):
    hi = x.astype(BF16)
    r1 = x - hi.astype(F32)
    mid = r1.astype(BF16)
    lo = (r1 - mid.astype(F32)).astype(BF16)
    d = lambda y: jnp.dot(tri, y, preferred_element_type=F32)
    return d(hi) + d(mid) + d(lo)


def _chunk_operands(r, v, kk, ld, kd, b, reverse):
    c = r.shape[0]
    ri = lax.broadcasted_iota(jnp.int32, (c, c), 0)
    ci = lax.broadcasted_iota(jnp.int32, (c, c), 1)
    tri = ((ci >= ri) if reverse else (ci <= ri)).astype(BF16)
    gam = _cumsum_rows(tri, ld)
    g_end = gam[0:1, :] if reverse else gam[c - 1:c, :]
    ginv = jnp.exp(-gam)
    dec = jnp.exp(g_end - gam)
    ops = {
        "at": -kk * jnp.exp(gam - ld), "bt": b * ginv, "kt": kd * ginv, "rt": r * jnp.exp(gam),
        "bh": b * dec, "kh": kd * dec, "v": v,
    }
    return {k: x.astype(BF16) for k, x in ops.items()}, jnp.exp(g_end)


def _scan_chunks(operands, s_refs, y_refs):
    c = CHUNK
    lane = lax.broadcasted_iota(jnp.int32, (c, LANES), 1)
    row = lax.broadcasted_iota(jnp.int32, (c, LANES), 0)
    lo_half = lane < RWKV_HEAD
    s_idx = lane % c
    eye2 = (s_idx == row).astype(F32)
    masks = [((s_idx > row), (s_idx >= row)) if rev else ((s_idx < row), (s_idx <= row)) for rev in (False, True)]
    bd_r = lax.broadcasted_iota(jnp.int32, (LANES, LANES), 0) // RWKV_HEAD
    bd_c = lax.broadcasted_iota(jnp.int32, (LANES, LANES), 1) // RWKV_HEAD
    blockdiag = bd_r == bd_c

    def sm(x):
        x = x.astype(BF16)
        zero = jnp.zeros_like(x)
        return jnp.concatenate([jnp.where(lo_half, x, zero), jnp.where(lo_half, zero, x)], axis=0)

    chains = [(d, pr) for d in range(2) for pr in range(RWKV_WIDTH // LANES)]
    op = lambda name, ch: operands[ch[0]][0][name][:, ch[1] * LANES:(ch[1] + 1) * LANES]

    aa = [_mm_nt(jnp.concatenate([op("at", ch), op("rt", ch)], axis=0),
                 jnp.concatenate([sm(op("bt", ch)), sm(op("kt", ch))], axis=0)) for ch in chains]
    strict = [masks[ch[0]][0] for ch in chains]
    incl = [masks[ch[0]][1] for ch in chains]
    a_ak = [jnp.where(m, x[:c, LANES:], 0.0).astype(BF16) for m, x in zip(strict, aa)]
    a_rbk = [jnp.where(jnp.concatenate([m, m], axis=1), x[c:], 0.0).astype(BF16) for m, x in zip(incl, aa)]
    pw = [jnp.where(m, x[:c, :LANES], 0.0) for m, x in zip(strict, aa)]
    t_inv = [eye2 + x for x in pw]
    for _ in range(int(math.log2(c)) - 1):
        pw = [_mm(x.astype(BF16), sm(x)) for x in pw]
        t_inv = [t + _mm(t.astype(BF16), sm(x)) for t, x in zip(t_inv, pw)]
    av = [_mm(x, sm(op("v", ch))) for x, ch in zip(a_ak, chains)]
    wx = [_mm(t.astype(BF16), jnp.concatenate([sm(op("at", ch)), sm(x)], axis=1))
          for t, x, ch in zip(t_inv, av, chains)]
    s0 = [s_refs[d][pr] for d, pr in chains]
    us = [_mm_nt(jnp.concatenate([x[:, :LANES].astype(BF16), op("rt", ch)], axis=0), s.astype(BF16))
          for x, s, ch in zip(wx, s0, chains)]
    u16 = [(x[:c] + w[:, LANES:]).astype(BF16) for x, w in zip(us, wx)]
    upd = [_mm_tn(jnp.concatenate([u, op("v", ch)], axis=0), jnp.concatenate([op("bh", ch), op("kh", ch)], axis=0))
           for u, ch in zip(u16, chains)]
    for (d, pr), s, x in zip(chains, s0, upd):
        g_c = operands[d][1][:, pr * LANES:(pr + 1) * LANES]
        s_refs[d][pr] = s * g_c + jnp.where(blockdiag, x, 0.0)
    y = [x[c:] + _mm(a, jnp.concatenate([sm(u), sm(op("v", ch))], axis=0))
         for x, a, u, ch in zip(us, a_rbk, u16, chains)]
    for (d, pr), x in zip(chains, y):
        y_refs[d][:, pr * LANES:(pr + 1) * LANES] = x


def _scan_kernel(rf, vf, kkf, ldf, kdf, bf, rb, vb, kkb, ldb, kdb, bb, yf_ref, yb_ref, s_sc):
    @pl.when(pl.program_id(1) == 0)
    def _():
        s_sc[...] = jnp.zeros(s_sc.shape, F32)

    fwd = _chunk_operands(rf[...], vf[...], kkf[...], ldf[...], kdf[...], bf[...], False)
    bwd = _chunk_operands(rb[...], vb[...], kkb[...], ldb[...], kdb[...], bb[...], True)
    _scan_chunks((fwd, bwd), (s_sc.at[0], s_sc.at[1]), (yf_ref, yb_ref))


def _rwkv_scan(r, v, kk, ld0, ld1, kd0, kd1, b0, b1, batch, seq_len):
    n = r.shape[0]
    c = CHUNK
    nc = seq_len // c
    w = RWKV_WIDTH
    fwd = pl.BlockSpec((c, w), lambda b, i: (b * nc + i, 0))
    bwd = pl.BlockSpec((c, w), lambda b, i: (b * nc + nc - 1 - i, 0))
    return pl.pallas_call(
        _scan_kernel,
        grid=(batch, nc),
        in_specs=[fwd] * 6 + [bwd] * 6,
        out_specs=[fwd, bwd],
        out_shape=[jax.ShapeDtypeStruct((n, w), F32)] * 2,
        scratch_shapes=[pltpu.VMEM((2, w // LANES, LANES, LANES), F32)],
        compiler_params=_cparams(("parallel", "arbitrary")),
        name="rwkv_scan",
    )(r, v, kk, ld0, kd0, b0, r, v, kk, ld1, kd1, b1)


def _merge_kernel(x_ref, attn_ref, yf_ref, yb_ref, bonus_ref, g_ref, gate_ref, lnxw_ref, lnxb_ref,
                  seg_ref, wb0_ref, wb1_ref, wout_ref, gpost_ref, o_ref):
    seg = seg_ref[...]
    y = yf_ref[...] + yb_ref[...]
    mu = _seg_dot2(y, seg) * (1.0 / RWKV_HEAD)
    yc = y - mu
    var = _seg_dot2(yc * yc, seg) * (1.0 / RWKV_HEAD)
    yn = yc * lax.rsqrt(var + LNX_EPS) * lnxw_ref[...] + lnxb_ref[...]
    rw = ((yn + bonus_ref[...]) * g_ref[...]).astype(BF16)
    br0 = lax.dot_general(attn_ref[...], wb0_ref[...], (((0,), (0,)), ((), ())), preferred_element_type=F32)
    br1 = jnp.dot(rw, wb1_ref[...], preferred_element_type=F32)
    gates = gate_ref[...]
    merged = gates[:, :D_MODEL].astype(F32) * br0 + gates[:, D_MODEL:].astype(F32) * br1
    mo = jnp.dot(merged.astype(BF16), wout_ref[...], preferred_element_type=F32)
    ms = jnp.mean(mo * mo, axis=-1, keepdims=True)
    o_ref[...] = x_ref[...] + mo * lax.rsqrt(ms + NORM_EPS) * gpost_ref[...]


def _merge(x, attn, yf, yb, bonus, g, gates, lnxw, lnxb, seg, wb0, wb1, wout, gpost):
    n = x.shape[0]
    tm = TM_MERGE
    w = RWKV_WIDTH
    tok = lambda i: (i, 0)
    return pl.pallas_call(
        _merge_kernel,
        grid=(n // tm,),
        in_specs=[
            pl.BlockSpec((tm, D_MODEL), tok),
            pl.BlockSpec((ATTN_WIDTH, tm), lambda i: (0, i)),
            pl.BlockSpec((tm, w), tok),
            pl.BlockSpec((tm, w), tok),
            pl.BlockSpec((tm, w), tok),
            pl.BlockSpec((tm, w), tok),
            pl.BlockSpec((tm, 2 * D_MODEL), tok),
            _full((1, w)),
            _full((1, w)),
            _full((w, w)),
            _full((ATTN_WIDTH, D_MODEL)),
            _full((w, D_MODEL)),
            _full((D_MODEL, D_MODEL)),
            _full((1, D_MODEL)),
        ],
        out_specs=pl.BlockSpec((tm, D_MODEL), tok),
        out_shape=jax.ShapeDtypeStruct((n, D_MODEL), F32),
        compiler_params=_cparams(("parallel",)),
        name="merge",
    )(x, attn, yf, yb, bonus, g, gates, lnxw, lnxb, seg, wb0, wb1, wout, gpost)


def _ffn_kernel(x_ref, gpre_ref, wg_ref, wu_ref, wo_ref, gpost_ref, o_ref, h_sc, acc_sc):
    j = pl.program_id(1)

    @pl.when(j == 0)
    def _():
        x = x_ref[...]
        ms = jnp.mean(x * x, axis=-1, keepdims=True)
        h_sc[...] = (x * lax.rsqrt(ms + NORM_EPS) * gpre_ref[...]).astype(BF16)
        acc_sc[...] = jnp.zeros(acc_sc.shape, F32)

    h = h_sc[...]
    gg = jnp.dot(h, wg_ref[...], preferred_element_type=F32)
    uu = jnp.dot(h, wu_ref[...], preferred_element_type=F32)
    f = (gg * _sigmoid(gg) * uu).astype(BF16)
    acc_sc[...] += jnp.dot(f, wo_ref[...], preferred_element_type=F32)

    @pl.when(j == pl.num_programs(1) - 1)
    def _():
        acc = acc_sc[...]
        ms = jnp.mean(acc * acc, axis=-1, keepdims=True)
        o_ref[...] = x_ref[...] + acc * lax.rsqrt(ms + NORM_EPS) * gpost_ref[...]


def _ffn(x, gpre, w_in_bf, w_out_bf, gpost):
    n = x.shape[0]
    tm, tf = TM_FFN, TF_FFN
    nf = D_FF // tf
    return pl.pallas_call(
        _ffn_kernel,
        grid=(n // tm, nf),
        in_specs=[
            pl.BlockSpec((tm, D_MODEL), lambda i, j: (i, 0)),
            _full((1, D_MODEL)),
            pl.BlockSpec((D_MODEL, tf), lambda i, j: (0, j)),
            pl.BlockSpec((D_MODEL, tf), lambda i, j: (0, nf + j)),
            pl.BlockSpec((tf, D_MODEL), lambda i, j: (j, 0)),
            _full((1, D_MODEL)),
        ],
        out_specs=pl.BlockSpec((tm, D_MODEL), lambda i, j: (i, 0)),
        out_shape=jax.ShapeDtypeStruct((n, D_MODEL), F32),
        scratch_shapes=[pltpu.VMEM((tm, D_MODEL), BF16), pltpu.VMEM((tm, D_MODEL), F32)],
        compiler_params=_cparams(("parallel", "arbitrary")),
        name="ffn",
    )(x, gpre, w_in_bf, w_in_bf, w_out_bf, gpost)


def _rope_tables(n):
    rows = n // GRID_W
    row = jnp.broadcast_to(jnp.arange(rows)[:, None], (rows, GRID_W)).reshape(-1).astype(F32)
    col = jnp.broadcast_to(jnp.arange(GRID_W)[None, :], (rows, GRID_W)).reshape(-1).astype(F32)
    nf = HEAD_DIM // 4
    inv = ROPE_THETA ** (-jnp.arange(nf, dtype=F32) / nf)
    ang = jnp.concatenate([row[:, None] * inv, col[:, None] * inv], axis=-1)
    cos = jnp.repeat(jnp.cos(ang), 2, axis=-1)
    sin = jnp.repeat(jnp.sin(ang), 2, axis=-1)
    sign = jnp.tile(jnp.array([-1.0, 1.0], F32), HEAD_DIM // 2)
    reps = LANES // HEAD_DIM
    return jnp.tile(cos, (1, reps)), jnp.tile(sin * sign, (1, reps))


def _block_diag2(m):
    z = jnp.zeros_like(m[0])
    return jnp.concatenate([jnp.concatenate([m[0], z], axis=1), jnp.concatenate([z, m[1]], axis=1)], axis=0)


def _trunk(x3, prm):
    batch, seq_len, _ = x3.shape
    n = batch * seq_len
    x = x3.reshape(n, D_MODEL)
    cos, sin = _rope_tables(seq_len)
    q, k, v, p, gates = _in_proj(x, prm["ln_mix_pre"], prm["w_in"], prm["q_gain"], prm["k_gain"],
                                 cos, sin, prm["seg"], seq_len)
    attn = _attention(q, k, v, batch, seq_len)
    r, vv, kk, ld0, ld1, kd0, kd1, b0, b1, bonus, g = _rwkv_prep(
        p, prm["shift_mu"], prm["w_up"], prm["a_up"], prm["g_up"], prm["w0"], prm["a0"],
        prm["k_k"], prm["k_a"], prm["r_k"], prm["seg"], seq_len)
    yf, yb = _rwkv_scan(r, vv, kk, ld0, ld1, kd0, kd1, b0, b1, batch, seq_len)
    x1 = _merge(x, attn, yf, yb, bonus, g, gates, prm["lnx_w"], prm["lnx_b"], prm["seg"],
                prm["wb0"], prm["wb1"], prm["w_out"], prm["ln_mix_post"])
    x2 = _ffn(x1, prm["ln_ffn_pre"], prm["w_ffn_in"], prm["w_ffn_out"], prm["ln_ffn_post"])
    return x2.reshape(batch, seq_len, D_MODEL)


def _prepare_params(ln_mix_pre, ln_mix_post, ln_ffn_pre, ln_ffn_post, w_in, q_gain, k_gain, shift_mu,
                    w0, w_up, a0, a_up, g_up, k_k, k_a, r_k, lnx_w, lnx_b, w_branch, w_out, w_ffn_in, w_ffn_out):
    row = lambda t: t.reshape(1, -1).astype(F32)
    head = jnp.arange(RWKV_WIDTH) // RWKV_HEAD
    seg = (head[:, None] == head[None, :]).astype(BF16)
    return {
        "ln_mix_pre": row(ln_mix_pre), "ln_mix_post": row(ln_mix_post),
        "ln_ffn_pre": row(ln_ffn_pre), "ln_ffn_post": row(ln_ffn_post),
        "w_in": w_in.astype(BF16),
        "q_gain": row(jnp.tile(q_gain, ATTN_HEADS)), "k_gain": row(jnp.tile(k_gain, ATTN_KV_HEADS)),
        "shift_mu": row(shift_mu),
        "w0": row(w0), "a0": row(a0),
        "w_up": _block_diag2(w_up).astype(BF16), "a_up": _block_diag2(a_up).astype(BF16),
        "g_up": g_up.astype(BF16),
        "k_k": row(k_k), "k_a": row(k_a), "r_k": row(r_k),
        "lnx_w": row(lnx_w), "lnx_b": row(lnx_b),
        "wb0": w_branch[0].astype(BF16), "wb1": w_branch[1].astype(BF16),
        "w_out": w_out.astype(BF16),
        "w_ffn_in": w_ffn_in.astype(BF16), "w_ffn_out": w_ffn_out.astype(BF16),
        "seg": seg,
    }


def kernel(x_prompt, x_sample, ln_mix_pre, ln_mix_post, ln_ffn_pre, ln_ffn_post, w_in, q_gain, k_gain, shift_mu,
           w0, w_up, a0, a_up, g_up, k_k, k_a, r_k, lnx_w, lnx_b, w_branch, w_out, w_ffn_in, w_ffn_out):
    weights = (ln_mix_pre, ln_mix_post, ln_ffn_pre, ln_ffn_post, w_in, q_gain, k_gain, shift_mu,
               w0, w_up, a0, a_up, g_up, k_k, k_a, r_k, lnx_w, lnx_b, w_branch, w_out, w_ffn_in, w_ffn_out)
    assert all(t.shape[0] == 1 for t in weights), "single-layer stack expected"
    prm = _prepare_params(*(t[0] for t in weights))
    return (_trunk(x_prompt, prm), _trunk(x_sample, prm))
```

```python
import functools
import math

import jax
import jax.numpy as jnp
from jax import lax
from jax.experimental import pallas as pl
from jax.experimental.pallas import tpu as pltpu

F32 = jnp.float32
BF16 = jnp.bfloat16

D_MODEL = 1024
GRID_W = 64
HEAD_DIM = 64
ATTN_HEADS = 8
ATTN_KV_HEADS = 2
ATTN_GROUP = ATTN_HEADS // ATTN_KV_HEADS
ATTN_WIDTH = ATTN_HEADS * HEAD_DIM
KV_WIDTH = ATTN_KV_HEADS * HEAD_DIM
ROPE_THETA = 10000.0
RWKV_HEADS = 8
RWKV_HEAD = 64
RWKV_WIDTH = RWKV_HEADS * RWKV_HEAD
DECAY_LORA = 64
AAA_LORA = 64
GATE_LORA = 128
LNX_EPS = 64e-5
D_FF = 2816
NORM_EPS = 1e-6

OFF_Q = 0
OFF_K = OFF_Q + ATTN_WIDTH
OFF_V = OFF_K + KV_WIDTH
OFF_R = OFF_V + KV_WIDTH
OFF_WD = OFF_R + 3 * RWKV_WIDTH
OFF_AD = OFF_WD + 2 * DECAY_LORA
OFF_GD = OFF_AD + 2 * AAA_LORA
OFF_GATE = OFF_GD + GATE_LORA
IN_COLS = OFF_GATE + 2 * D_MODEL
SHIFT_COLS = OFF_GATE - OFF_R

LANES = 128
VMEM_LIMIT = 56 * 1024 * 1024

TM_IN = 512
TQ = 256
TK = 2048
TM_PREP = 256
CHUNK = 64
SCAN_CHUNKS = 4
TM_MERGE = 512
TM_FFN = 512
TF_FFN = D_FF // 2


def _cparams(sem):
    return pltpu.CompilerParams(dimension_semantics=sem, vmem_limit_bytes=VMEM_LIMIT)


def _sigmoid(x):
    return 1.0 / (1.0 + jnp.exp(-x))


def _full(shape):
    nd = len(shape)
    return pl.BlockSpec(shape, lambda *_: (0,) * nd)


def _seg_dot2(x, seg):
    hi = x.astype(BF16)
    lo = (x - hi.astype(F32)).astype(BF16)
    return (jnp.dot(hi, seg, preferred_element_type=F32) + jnp.dot(lo, seg, preferred_element_type=F32))


def _in_proj_kernel(x_ref, g_ref, w_ref, qg_ref, kg_ref, cos_ref, sin_ref, seg_ref,
                    q_ref, k_ref, v_ref, p_ref, gate_ref):
    x = x_ref[...]
    ms = jnp.mean(x * x, axis=-1, keepdims=True)
    h = (x * lax.rsqrt(ms + NORM_EPS) * g_ref[...]).astype(BF16)
    cos = cos_ref[...]
    sin = sin_ref[...]
    tm = x.shape[0]
    even = (lax.broadcasted_iota(jnp.int32, (tm, LANES), 1) % 2) == 0

    def head_norm_rope(t, gain, seg):
        ss = jnp.dot((t * t).astype(BF16), seg, preferred_element_type=F32) * (1.0 / HEAD_DIM)
        tn = t * lax.rsqrt(ss + NORM_EPS) * gain
        outs = []
        for j in range(t.shape[1] // LANES):
            tj = tn[:, j * LANES:(j + 1) * LANES]
            partner = jnp.where(even, pltpu.roll(tj, LANES - 1, axis=1), pltpu.roll(tj, 1, axis=1))
            outs.append(tj * cos + partner * sin)
        return outs

    q = jnp.dot(h, w_ref[:, OFF_Q:OFF_K], preferred_element_type=F32)
    qr = head_norm_rope(q, qg_ref[...], seg_ref[...])
    scale = HEAD_DIM ** -0.5 * math.log2(math.e)
    for j, qj in enumerate(qr):
        qj = (qj * scale).astype(BF16)
        q_ref[2 * j] = qj[:, :HEAD_DIM]
        q_ref[2 * j + 1] = qj[:, HEAD_DIM:]

    kv = jnp.dot(h, w_ref[:, OFF_K:OFF_R], preferred_element_type=F32)
    kr = head_norm_rope(kv[:, :KV_WIDTH], kg_ref[...], seg_ref[:KV_WIDTH, :KV_WIDTH])[0].astype(BF16)
    k_ref[0] = kr[:, :HEAD_DIM]
    k_ref[1] = kr[:, HEAD_DIM:]
    vv = kv[:, KV_WIDTH:].astype(BF16)
    ones = jnp.ones((tm, HEAD_DIM), BF16)
    v_ref[0] = jnp.concatenate([vv[:, :HEAD_DIM], ones], axis=1)
    v_ref[1] = jnp.concatenate([vv[:, HEAD_DIM:], ones], axis=1)

    step = 640
    for c in range(SHIFT_COLS // step):
        p_ref[:, c * step:(c + 1) * step] = jnp.dot(
            h, w_ref[:, OFF_R + c * step:OFF_R + (c + 1) * step], preferred_element_type=F32)
    step = 512
    for c in range(2 * D_MODEL // step):
        gt = jnp.dot(h, w_ref[:, OFF_GATE + c * step:OFF_GATE + (c + 1) * step], preferred_element_type=F32)
        gate_ref[:, c * step:(c + 1) * step] = _sigmoid(gt).astype(BF16)


def _in_proj(x, g, w_bf, qg, kg, cos, sin, seg, seq_len):
    n = x.shape[0]
    tm = TM_IN
    tiles_per_seq = seq_len // tm
    tok = lambda i: (i, 0)
    return pl.pallas_call(
        _in_proj_kernel,
        grid=(n // tm,),
        in_specs=[
            pl.BlockSpec((tm, D_MODEL), tok),
            _full((1, D_MODEL)),
            _full((D_MODEL, IN_COLS)),
            _full((1, ATTN_WIDTH)),
            _full((1, KV_WIDTH)),
            pl.BlockSpec((tm, LANES), lambda i: (i % tiles_per_seq, 0)),
            pl.BlockSpec((tm, LANES), lambda i: (i % tiles_per_seq, 0)),
            _full((ATTN_WIDTH, ATTN_WIDTH)),
        ],
        out_specs=[
            pl.BlockSpec((ATTN_HEADS, tm, HEAD_DIM), lambda i: (0, i, 0)),
            pl.BlockSpec((ATTN_KV_HEADS, tm, HEAD_DIM), lambda i: (0, i, 0)),
            pl.BlockSpec((ATTN_KV_HEADS, tm, LANES), lambda i: (0, i, 0)),
            pl.BlockSpec((tm, SHIFT_COLS), tok),
            pl.BlockSpec((tm, 2 * D_MODEL), tok),
        ],
        out_shape=[
            jax.ShapeDtypeStruct((ATTN_HEADS, n, HEAD_DIM), BF16),
            jax.ShapeDtypeStruct((ATTN_KV_HEADS, n, HEAD_DIM), BF16),
            jax.ShapeDtypeStruct((ATTN_KV_HEADS, n, LANES), BF16),
            jax.ShapeDtypeStruct((n, SHIFT_COLS), F32),
            jax.ShapeDtypeStruct((n, 2 * D_MODEL), BF16),
        ],
        compiler_params=_cparams(("parallel",)),
        name="in_proj",
    )(x, g, w_bf, qg, kg, cos, sin, seg)


def _attn_kernel(q_ref, k_ref, v_ref, o_ref, m_sc, acc_sc):
    ki = pl.program_id(3)

    @pl.when(ki == 0)
    def _():
        m_sc[...] = jnp.full(m_sc.shape, -jnp.inf, F32)
        acc_sc[...] = jnp.zeros(acc_sc.shape, F32)

    g, tq, hd = q_ref.shape
    k = k_ref[...]
    v = v_ref[...]
    scores = lambda i: lax.dot_general(q_ref[i], k, (((1,), (1,)), ((), ())), preferred_element_type=F32)

    s = scores(0)
    for i in range(g):
        s_next = scores(i + 1) if i + 1 < g else None
        rows = slice(i * tq, (i + 1) * tq)
        m_prev = m_sc[rows]
        m_next = jnp.maximum(m_prev, jnp.max(s, axis=1, keepdims=True))
        alpha = jnp.exp2(m_prev - m_next)
        p = jnp.exp2((s - pltpu.repeat(m_next, s.shape[1] // LANES, axis=1)).astype(BF16))
        acc_sc[rows] = acc_sc[rows] * alpha + jnp.dot(p, v, preferred_element_type=F32)
        m_sc[rows] = m_next
        s = s_next

    @pl.when(ki == pl.num_programs(3) - 1)
    def _():
        acc = acc_sc[...]
        o = acc / pltpu.roll(acc, HEAD_DIM, axis=1)
        o_ref[...] = jnp.concatenate([o[i * tq:(i + 1) * tq, :hd] for i in range(g)], axis=1).astype(o_ref.dtype)


def _attention(q, k, v, batch, seq_len):
    n = q.shape[1]
    nq = seq_len // TQ
    nk = seq_len // TK
    rows = ATTN_GROUP * TQ
    return pl.pallas_call(
        _attn_kernel,
        grid=(batch, ATTN_KV_HEADS, nq, nk),
        in_specs=[
            pl.BlockSpec((ATTN_GROUP, TQ, HEAD_DIM), lambda b, h, qi, ki: (h, b * nq + qi, 0)),
            pl.BlockSpec((None, TK, HEAD_DIM), lambda b, h, qi, ki: (h, b * nk + ki, 0)),
            pl.BlockSpec((None, TK, LANES), lambda b, h, qi, ki: (h, b * nk + ki, 0)),
        ],
        out_specs=pl.BlockSpec((TQ, ATTN_GROUP * HEAD_DIM), lambda b, h, qi, ki: (b * nq + qi, h)),
        out_shape=jax.ShapeDtypeStruct((n, ATTN_WIDTH), BF16),
        scratch_shapes=[pltpu.VMEM((rows, LANES), F32), pltpu.VMEM((rows, LANES), F32)],
        compiler_params=_cparams(("parallel", "parallel", "parallel", "arbitrary")),
        name="attention",
    )(q, k, v)


def _prep_kernel(p_ref, pprev_ref, pnext_ref, mu_ref, wup_ref, aup_ref, gup_ref, w0_ref, a0_ref,
                 kkw_ref, ka_ref, rk_ref, seg_ref,
                 r_o, v_o, kk_o, ld0_o, ld1_o, kd0_o, kd1_o, b0_o, b1_o, bonus_o, g_o, *, tiles_per_seq):
    pos = pl.program_id(0) % tiles_per_seq
    p = p_ref[...]
    tm = p.shape[0]
    row = lax.broadcasted_iota(jnp.int32, (tm, 1), 0)
    prev_row = jnp.where(pos == 0, 0.0, pprev_ref[7:8, :])
    next_row = jnp.where(pos == tiles_per_seq - 1, 0.0, pnext_ref[0:1, :])
    prev = jnp.where(row == 0, prev_row, pltpu.roll(p, 1, axis=0))
    nxt = jnp.where(row == tm - 1, next_row, pltpu.roll(p, tm - 1, axis=0))
    z = p + (0.5 * (prev + nxt) - p) * mu_ref[...]

    w = RWKV_WIDTH
    r = z[:, 0:w]
    k = z[:, w:2 * w]
    v = z[:, 2 * w:3 * w]
    wd = z[:, 3 * w:3 * w + 2 * DECAY_LORA]
    ad = z[:, 3 * w + 2 * DECAY_LORA:3 * w + 2 * DECAY_LORA + 2 * AAA_LORA]
    gd = z[:, 3 * w + 2 * DECAY_LORA + 2 * AAA_LORA:]

    wpre = w0_ref[...] + jnp.dot(jnp.tanh(wd).astype(BF16), wup_ref[...], preferred_element_type=F32)
    ld = -math.exp(-0.5) * _sigmoid(wpre)
    a = _sigmoid(a0_ref[...] + jnp.dot(ad.astype(BF16), aup_ref[...], preferred_element_type=F32))
    g = jnp.dot(_sigmoid(gd).astype(BF16), gup_ref[...], preferred_element_type=F32)

    seg = seg_ref[...]
    kk = k * kkw_ref[...]
    kk = kk / jnp.maximum(jnp.sqrt(_seg_dot2(kk * kk, seg)), 1e-12)
    ka = ka_ref[...]
    kd0 = k * (1.0 + (a[:, :w] - 1.0) * ka)
    kd1 = k * (1.0 + (a[:, w:] - 1.0) * ka)
    bonus = _seg_dot2(r * (0.5 * (kd0 + kd1)) * rk_ref[...], seg) * v

    ld0_o[...] = ld[:, :w]
    ld1_o[...] = ld[:, w:]
    for ref, val in ((r_o, r), (v_o, v), (kk_o, kk), (kd0_o, kd0), (kd1_o, kd1), (b0_o, kk * a[:, :w]),
                     (b1_o, kk * a[:, w:]), (bonus_o, bonus), (g_o, g)):
        ref[...] = val.astype(ref.dtype)


def _rwkv_prep(p, mu, wup, aup, gup, w0, a0, kkw, ka, rk, seg, seq_len):
    n = p.shape[0]
    tm = TM_PREP
    tiles_per_seq = seq_len // tm
    rows8 = tm // 8
    last8 = n // 8 - 1
    tok = lambda i: (i, 0)
    w = RWKV_WIDTH
    out_spec = pl.BlockSpec((tm, w), tok)
    return pl.pallas_call(
        functools.partial(_prep_kernel, tiles_per_seq=tiles_per_seq),
        grid=(n // tm,),
        in_specs=[
            pl.BlockSpec((tm, SHIFT_COLS), tok),
            pl.BlockSpec((8, SHIFT_COLS), lambda i: (jnp.maximum(i * rows8 - 1, 0), 0)),
            pl.BlockSpec((8, SHIFT_COLS), lambda i: (jnp.minimum((i + 1) * rows8, last8), 0)),
            _full((1, SHIFT_COLS)),
            _full((2 * DECAY_LORA, 2 * w)),
            _full((2 * AAA_LORA, 2 * w)),
            _full((GATE_LORA, w)),
            _full((1, 2 * w)),
            _full((1, 2 * w)),
            _full((1, w)),
            _full((1, w)),
            _full((1, w)),
            _full((w, w)),
        ],
        out_specs=[out_spec] * 11,
        out_shape=[jax.ShapeDtypeStruct((n, w), F32 if i in (3, 4) else BF16) for i in range(11)],
        compiler_params=_cparams(("parallel",)),
        name="rwkv_prep",
    )(p, p, p, mu, wup, aup, gup, w0, a0, kkw, ka, rk, seg)


def _mm(a, b):
    return jnp.dot(a, b, preferred_element_type=F32)


def _mm_nt(a, b):
    return lax.dot_general(a, b, (((1,), (1,)), ((), ())), preferred_element_type=F32)


def _mm_tn(a, b):
    return lax.dot_general(a, b, (((0,), (0,)), ((), ())), preferred_element_type=F32)


def _cumsum_rows(tri, x):
    hi = x.astype(BF16)
    r1 = x - hi.astype(F32)
    mid = r1.astype(BF16)
    lo = (r1 - mid.astype(F32)).astype(BF16)
    d = lambda y: jnp.dot(tri, y, preferred_element_type=F32)
    return d(hi) + d(mid) + d(lo)


def _chunk_operands(r, v, kk, ld, kd, b, reverse):
    c = r.shape[0]
    ri = lax.broadcasted_iota(jnp.int32, (c, c), 0)
    ci = lax.broadcasted_iota(jnp.int32, (c, c), 1)
    tri = ((ci >= ri) if reverse else (ci <= ri)).astype(BF16)
    gam = _cumsum_rows(tri, ld)
    g_end = gam[0:1, :] if reverse else gam[c - 1:c, :]
    ginv = jnp.exp(-gam)
    dec = jnp.exp(g_end - gam)
    ops = {
        "at": -kk * jnp.exp(gam - ld), "bt": b * ginv, "kt": kd * ginv, "rt": r * jnp.exp(gam),
        "bh": b * dec, "kh": kd * dec, "v": v,
    }
    return {k: x.astype(BF16) for k, x in ops.items()}, jnp.exp(g_end)


def _scan_block(operands, order, s_refs, y_refs):
    c = CHUNK
    n_pairs = RWKV_WIDTH // LANES
    lane = lax.broadcasted_iota(jnp.int32, (c, LANES), 1)
    row = lax.broadcasted_iota(jnp.int32, (c, LANES), 0)
    lo_half = lane < RWKV_HEAD
    s_idx = lane % c
    eye2 = (s_idx == row).astype(F32)
    masks = [((s_idx > row), (s_idx >= row)) if rev else ((s_idx < row), (s_idx <= row)) for rev in (False, True)]
    bd_r = lax.broadcasted_iota(jnp.int32, (LANES, LANES), 0) // RWKV_HEAD
    bd_c = lax.broadcasted_iota(jnp.int32, (LANES, LANES), 1) // RWKV_HEAD
    blockdiag = bd_r == bd_c

    def sm(x):
        x = x.astype(BF16)
        zero = jnp.zeros_like(x)
        return jnp.concatenate([jnp.where(lo_half, x, zero), jnp.where(lo_half, zero, x)], axis=0)

    chains = [(d, j, pr) for d in range(2) for j in order[d] for pr in range(n_pairs)]
    op = lambda name, ch: operands[ch[:2]][0][name][:, ch[2] * LANES:(ch[2] + 1) * LANES]

    aa = [_mm_nt(jnp.concatenate([op("at", ch), op("rt", ch)], axis=0),
                 jnp.concatenate([sm(op("bt", ch)), sm(op("kt", ch))], axis=0)) for ch in chains]
    strict = [masks[ch[0]][0] for ch in chains]
    incl = [masks[ch[0]][1] for ch in chains]
    a_ak = [jnp.where(m, x[:c, LANES:], 0.0).astype(BF16) for m, x in zip(strict, aa)]
    a_rb = [jnp.where(m, x[c:, :LANES], 0.0).astype(BF16) for m, x in zip(incl, aa)]
    a_rk = [jnp.where(m, x[c:, LANES:], 0.0).astype(BF16) for m, x in zip(incl, aa)]
    avk = [_mm(jnp.concatenate([x, z], axis=0), sm(op("v", ch))) for x, z, ch in zip(a_ak, a_rk, chains)]
    pw = [jnp.where(m, x[:c, :LANES], 0.0) for m, x in zip(strict, aa)]
    t_inv = [eye2 + x for x in pw]
    pw = [_mm(x.astype(BF16), sm(x)) for x in pw]
    for _ in range(int(math.log2(c)) - 2):
        tp = [_mm(jnp.concatenate([t.astype(BF16), x.astype(BF16)], axis=0), sm(x)) for t, x in zip(t_inv, pw)]
        t_inv = [t + z[:c] for t, z in zip(t_inv, tp)]
        pw = [z[c:] for z in tp]
    t_inv = [t + _mm(t.astype(BF16), sm(x)) for t, x in zip(t_inv, pw)]
    wx = [_mm(t.astype(BF16), jnp.concatenate([sm(op("at", ch)), sm(x[:c])], axis=1))
          for t, x, ch in zip(t_inv, avk, chains)]
    pre = {ch: (w[:, :LANES].astype(BF16), w[:, LANES:], a, x[c:]) for ch, w, a, x in zip(chains, wx, a_rb, avk)}

    state = {(d, pr): s_refs[d][pr] for d in range(2) for pr in range(n_pairs)}
    for step in range(len(order[0])):
        tail = [(d, order[d][step], pr) for d in range(2) for pr in range(n_pairs)]
        s0 = [state[(d, pr)] for d, _, pr in tail]
        us = [_mm_nt(jnp.concatenate([pre[ch][0], op("rt", ch)], axis=0), s.astype(BF16)) for ch, s in zip(tail, s0)]
        u16 = [(x[:c] + pre[ch][1]).astype(BF16) for x, ch in zip(us, tail)]
        upd = [_mm_tn(jnp.concatenate([u, op("v", ch)], axis=0),
                      jnp.concatenate([op("bh", ch), op("kh", ch)], axis=0)) for u, ch in zip(u16, tail)]
        for ch, s, x in zip(tail, s0, upd):
            g_c = operands[ch[:2]][1][:, ch[2] * LANES:(ch[2] + 1) * LANES]
            state[(ch[0], ch[2])] = s * g_c + jnp.where(blockdiag, x, 0.0)
        y = [x[c:] + pre[ch][3] + _mm(pre[ch][2], sm(u)) for x, u, ch in zip(us, u16, tail)]
        for (d, j, pr), x in zip(tail, y):
            y_refs[d][j * c:(j + 1) * c, pr * LANES:(pr + 1) * LANES] = x
    for (d, pr), s in state.items():
        s_refs[d][pr] = s


def _scan_kernel(rf, vf, kkf, ldf, kdf, bf, rb, vb, kkb, ldb, kdb, bb, yf_ref, yb_ref, s_sc):
    @pl.when(pl.program_id(1) == 0)
    def _():
        s_sc[...] = jnp.zeros(s_sc.shape, F32)

    c = CHUNK
    n_chunks = rf.shape[0] // c
    refs = ((rf, vf, kkf, ldf, kdf, bf), (rb, vb, kkb, ldb, kdb, bb))
    operands = {(d, j): _chunk_operands(*(x[j * c:(j + 1) * c, :].astype(F32) for x in refs[d]), reverse=bool(d))
                for d in range(2) for j in range(n_chunks)}
    order = (list(range(n_chunks)), list(range(n_chunks - 1, -1, -1)))
    _scan_block(operands, order, (s_sc.at[0], s_sc.at[1]), (yf_ref, yb_ref))


def _rwkv_scan(r, v, kk, ld0, ld1, kd0, kd1, b0, b1, batch, seq_len):
    n = r.shape[0]
    rows = SCAN_CHUNKS * CHUNK
    nb = seq_len // rows
    w = RWKV_WIDTH
    fwd = pl.BlockSpec((rows, w), lambda b, i: (b * nb + i, 0))
    bwd = pl.BlockSpec((rows, w), lambda b, i: (b * nb + nb - 1 - i, 0))
    return pl.pallas_call(
        _scan_kernel,
        grid=(batch, nb),
        in_specs=[fwd] * 6 + [bwd] * 6,
        out_specs=[fwd, bwd],
        out_shape=[jax.ShapeDtypeStruct((n, w), F32)] * 2,
        scratch_shapes=[pltpu.VMEM((2, w // LANES, LANES, LANES), F32)],
        compiler_params=_cparams(("parallel", "arbitrary")),
        name="rwkv_scan",
    )(r, v, kk, ld0, kd0, b0, r, v, kk, ld1, kd1, b1)


def _merge_kernel(x_ref, attn_ref, yf_ref, yb_ref, bonus_ref, g_ref, gate_ref, lnxw_ref, lnxb_ref,
                  seg_ref, wb0_ref, wb1_ref, wout_ref, gpost_ref, o_ref):
    seg = seg_ref[...]
    y = yf_ref[...] + yb_ref[...]
    mu = _seg_dot2(y, seg) * (1.0 / RWKV_HEAD)
    yc = y - mu
    var = _seg_dot2(yc * yc, seg) * (1.0 / RWKV_HEAD)
    yn = yc * lax.rsqrt(var + LNX_EPS) * lnxw_ref[...] + lnxb_ref[...]
    rw = ((yn + bonus_ref[...].astype(F32)) * g_ref[...].astype(F32)).astype(BF16)
    br0 = jnp.dot(attn_ref[...], wb0_ref[...], preferred_element_type=F32)
    br1 = jnp.dot(rw, wb1_ref[...], preferred_element_type=F32)
    gates = gate_ref[...]
    merged = gates[:, :D_MODEL].astype(F32) * br0 + gates[:, D_MODEL:].astype(F32) * br1
    mo = jnp.dot(merged.astype(BF16), wout_ref[...], preferred_element_type=F32)
    ms = jnp.mean(mo * mo, axis=-1, keepdims=True)
    o_ref[...] = x_ref[...] + mo * lax.rsqrt(ms + NORM_EPS) * gpost_ref[...]


def _merge(x, attn, yf, yb, bonus, g, gates, lnxw, lnxb, seg, wb0, wb1, wout, gpost):
    n = x.shape[0]
    tm = TM_MERGE
    w = RWKV_WIDTH
    tok = lambda i: (i, 0)
    return pl.pallas_call(
        _merge_kernel,
        grid=(n // tm,),
        in_specs=[
            pl.BlockSpec((tm, D_MODEL), tok),
            pl.BlockSpec((tm, ATTN_WIDTH), tok),
            pl.BlockSpec((tm, w), tok),
            pl.BlockSpec((tm, w), tok),
            pl.BlockSpec((tm, w), tok),
            pl.BlockSpec((tm, w), tok),
            pl.BlockSpec((tm, 2 * D_MODEL), tok),
            _full((1, w)),
            _full((1, w)),
            _full((w, w)),
            _full((ATTN_WIDTH, D_MODEL)),
            _full((w, D_MODEL)),
            _full((D_MODEL, D_MODEL)),
            _full((1, D_MODEL)),
        ],
        out_specs=pl.BlockSpec((tm, D_MODEL), tok),
        out_shape=jax.ShapeDtypeStruct((n, D_MODEL), F32),
        compiler_params=_cparams(("parallel",)),
        name="merge",
    )(x, attn, yf, yb, bonus, g, gates, lnxw, lnxb, seg, wb0, wb1, wout, gpost)


def _ffn_kernel(x_ref, gpre_ref, wg_ref, wu_ref, wo_ref, gpost_ref, o_ref, h_sc, acc_sc):
    j = pl.program_id(1)

    @pl.when(j == 0)
    def _():
        x = x_ref[...]
        ms = jnp.mean(x * x, axis=-1, keepdims=True)
        h_sc[...] = (x * lax.rsqrt(ms + NORM_EPS) * gpre_ref[...]).astype(BF16)
        acc_sc[...] = jnp.zeros(acc_sc.shape, F32)

    h = h_sc[...]
    gg = jnp.dot(h, wg_ref[...], preferred_element_type=F32)
    uu = jnp.dot(h, wu_ref[...], preferred_element_type=F32)
    f = (gg * _sigmoid(gg) * uu).astype(BF16)
    acc_sc[...] += jnp.dot(f, wo_ref[...], preferred_element_type=F32)

    @pl.when(j == pl.num_programs(1) - 1)
    def _():
        acc = acc_sc[...]
        ms = jnp.mean(acc * acc, axis=-1, keepdims=True)
        o_ref[...] = x_ref[...] + acc * lax.rsqrt(ms + NORM_EPS) * gpost_ref[...]


def _ffn(x, gpre, w_in_bf, w_out_bf, gpost):
    n = x.shape[0]
    tm, tf = TM_FFN, TF_FFN
    nf = D_FF // tf
    return pl.pallas_call(
        _ffn_kernel,
        grid=(n // tm, nf),
        in_specs=[
            pl.BlockSpec((tm, D_MODEL), lambda i, j: (i, 0)),
            _full((1, D_MODEL)),
            pl.BlockSpec((D_MODEL, tf), lambda i, j: (0, j)),
            pl.BlockSpec((D_MODEL, tf), lambda i, j: (0, nf + j)),
            pl.BlockSpec((tf, D_MODEL), lambda i, j: (j, 0)),
            _full((1, D_MODEL)),
        ],
        out_specs=pl.BlockSpec((tm, D_MODEL), lambda i, j: (i, 0)),
        out_shape=jax.ShapeDtypeStruct((n, D_MODEL), F32),
        scratch_shapes=[pltpu.VMEM((tm, D_MODEL), BF16), pltpu.VMEM((tm, D_MODEL), F32)],
        compiler_params=_cparams(("parallel", "arbitrary")),
        name="ffn",
    )(x, gpre, w_in_bf, w_in_bf, w_out_bf, gpost)


def _rope_tables(n):
    rows = n // GRID_W
    row = jnp.broadcast_to(jnp.arange(rows)[:, None], (rows, GRID_W)).reshape(-1).astype(F32)
    col = jnp.broadcast_to(jnp.arange(GRID_W)[None, :], (rows, GRID_W)).reshape(-1).astype(F32)
    nf = HEAD_DIM // 4
    inv = ROPE_THETA ** (-jnp.arange(nf, dtype=F32) / nf)
    ang = jnp.concatenate([row[:, None] * inv, col[:, None] * inv], axis=-1)
    cos = jnp.repeat(jnp.cos(ang), 2, axis=-1)
    sin = jnp.repeat(jnp.sin(ang), 2, axis=-1)
    sign = jnp.tile(jnp.array([-1.0, 1.0], F32), HEAD_DIM // 2)
    reps = LANES // HEAD_DIM
    return jnp.tile(cos, (1, reps)), jnp.tile(sin * sign, (1, reps))


def _block_diag2(m):
    z = jnp.zeros_like(m[0])
    return jnp.concatenate([jnp.concatenate([m[0], z], axis=1), jnp.concatenate([z, m[1]], axis=1)], axis=0)


def _trunk(x3, prm):
    batch, seq_len, _ = x3.shape
    for tile in (TM_IN, TQ, TK, TM_PREP, SCAN_CHUNKS * CHUNK, TM_MERGE, TM_FFN, GRID_W):
        assert seq_len % tile == 0, (seq_len, tile)
    n = batch * seq_len
    x = x3.reshape(n, D_MODEL)
    cos, sin = _rope_tables(seq_len)
    q, k, v, p, gates = _in_proj(x, prm["ln_mix_pre"], prm["w_in"], prm["q_gain"], prm["k_gain"],
                                 cos, sin, prm["seg"], seq_len)
    attn = _attention(q, k, v, batch, seq_len)
    r, vv, kk, ld0, ld1, kd0, kd1, b0, b1, bonus, g = _rwkv_prep(
        p, prm["shift_mu"], prm["w_up"], prm["a_up"], prm["g_up"], prm["w0"], prm["a0"],
        prm["k_k"], prm["k_a"], prm["r_k"], prm["seg"], seq_len)
    yf, yb = _rwkv_scan(r, vv, kk, ld0, ld1, kd0, kd1, b0, b1, batch, seq_len)
    x1 = _merge(x, attn, yf, yb, bonus, g, gates, prm["lnx_w"], prm["lnx_b"], prm["seg"],
                prm["wb0"], prm["wb1"], prm["w_out"], prm["ln_mix_post"])
    x2 = _ffn(x1, prm["ln_ffn_pre"], prm["w_ffn_in"], prm["w_ffn_out"], prm["ln_ffn_post"])
    return x2.reshape(batch, seq_len, D_MODEL)


def _prepare_params(ln_mix_pre, ln_mix_post, ln_ffn_pre, ln_ffn_post, w_in, q_gain, k_gain, shift_mu,
                    w0, w_up, a0, a_up, g_up, k_k, k_a, r_k, lnx_w, lnx_b, w_branch, w_out, w_ffn_in, w_ffn_out):
    row = lambda t: t.reshape(1, -1).astype(F32)
    head = jnp.arange(RWKV_WIDTH) // RWKV_HEAD
    seg = (head[:, None] == head[None, :]).astype(BF16)
    return {
        "ln_mix_pre": row(ln_mix_pre), "ln_mix_post": row(ln_mix_post),
        "ln_ffn_pre": row(ln_ffn_pre), "ln_ffn_post": row(ln_ffn_post),
        "w_in": w_in.astype(BF16),
        "q_gain": row(jnp.tile(q_gain, ATTN_HEADS)), "k_gain": row(jnp.tile(k_gain, ATTN_KV_HEADS)),
        "shift_mu": row(shift_mu),
        "w0": row(w0), "a0": row(a0),
        "w_up": _block_diag2(w_up).astype(BF16), "a_up": _block_diag2(a_up).astype(BF16),
        "g_up": g_up.astype(BF16),
        "k_k": row(k_k), "k_a": row(k_a), "r_k": row(r_k),
        "lnx_w": row(lnx_w), "lnx_b": row(lnx_b),
        "wb0": w_branch[0].astype(BF16), "wb1": w_branch[1].astype(BF16),
        "w_out": w_out.astype(BF16),
        "w_ffn_in": w_ffn_in.astype(BF16), "w_ffn_out": w_ffn_out.astype(BF16),
        "seg": seg,
    }


def kernel(x_prompt, x_sample, ln_mix_pre, ln_mix_post, ln_ffn_pre, ln_ffn_post, w_in, q_gain, k_gain, shift_mu,
           w0, w_up, a0, a_up, g_up, k_k, k_a, r_k, lnx_w, lnx_b, w_branch, w_out, w_ffn_in, w_ffn_out):
    weights = (ln_mix_pre, ln_mix_post, ln_ffn_pre, ln_ffn_post, w_in, q_gain, k_gain, shift_mu,
               w0, w_up, a0, a_up, g_up, k_k, k_a, r_k, lnx_w, lnx_b, w_branch, w_out, w_ffn_in, w_ffn_out)
    assert all(t.shape[0] == 1 for t in weights), "single-layer stack expected"
    prm = _prepare_params(*(t[0] for t in weights))
    return (_trunk(x_prompt, prm), _trunk(x_sample, prm))
```

```python
import functools
import math

import jax
import jax.numpy as jnp
from jax import lax
from jax.experimental import pallas as pl
from jax.experimental.pallas import tpu as pltpu

F32 = jnp.float32
BF16 = jnp.bfloat16

D_MODEL = 1024
GRID_W = 64
HEAD_DIM = 64
ATTN_HEADS = 8
ATTN_KV_HEADS = 2
ATTN_GROUP = ATTN_HEADS // ATTN_KV_HEADS
ATTN_WIDTH = ATTN_HEADS * HEAD_DIM
KV_WIDTH = ATTN_KV_HEADS * HEAD_DIM
ROPE_THETA = 10000.0
RWKV_HEADS = 8
RWKV_HEAD = 64
RWKV_WIDTH = RWKV_HEADS * RWKV_HEAD
DECAY_LORA = 64
AAA_LORA = 64
GATE_LORA = 128
LNX_EPS = 64e-5
D_FF = 2816
NORM_EPS = 1e-6

OFF_Q = 0
OFF_K = OFF_Q + ATTN_WIDTH
OFF_V = OFF_K + KV_WIDTH
OFF_R = OFF_V + KV_WIDTH
OFF_WD = OFF_R + 3 * RWKV_WIDTH
OFF_AD = OFF_WD + 2 * DECAY_LORA
OFF_GD = OFF_AD + 2 * AAA_LORA
OFF_GATE = OFF_GD + GATE_LORA
IN_COLS = OFF_GATE + 2 * D_MODEL
SHIFT_COLS = OFF_GATE - OFF_R

LANES = 128
VMEM_LIMIT = 56 * 1024 * 1024

TM_IN = 512
TQ = 256
TK = 2048
TM_PREP = 256
CHUNK = 64
SCAN_CHUNKS = 4
TM_MERGE = 512
TM_FFN = 512
TF_FFN = D_FF // 2


def _cparams(sem):
    return pltpu.CompilerParams(dimension_semantics=sem, vmem_limit_bytes=VMEM_LIMIT)


def _sigmoid(x):
    return 0.5 * jnp.tanh(0.5 * x) + 0.5


def _full(shape):
    nd = len(shape)
    return pl.BlockSpec(shape, lambda *_: (0,) * nd)


def _seg_dot2(x, seg):
    hi = x.astype(BF16)
    lo = (x - hi.astype(F32)).astype(BF16)
    return (jnp.dot(hi, seg, preferred_element_type=F32) + jnp.dot(lo, seg, preferred_element_type=F32))


def _in_proj_kernel(x_ref, g_ref, w_ref, qg_ref, kg_ref, cos_ref, sin_ref, seg_ref,
                    q_ref, k_ref, v_ref, p_ref, gate_ref):
    x = x_ref[...]
    ms = jnp.mean(x * x, axis=-1, keepdims=True)
    h = (x * lax.rsqrt(ms + NORM_EPS) * g_ref[...]).astype(BF16)
    cos = cos_ref[...]
    sin = sin_ref[...]
    tm = x.shape[0]
    even = (lax.broadcasted_iota(jnp.int32, (tm, LANES), 1) % 2) == 0

    def head_norm_rope(t, gain, seg):
        ss = jnp.dot((t * t).astype(BF16), seg, preferred_element_type=F32) * (1.0 / HEAD_DIM)
        tn = t * lax.rsqrt(ss + NORM_EPS) * gain
        outs = []
        for j in range(t.shape[1] // LANES):
            tj = tn[:, j * LANES:(j + 1) * LANES]
            partner = jnp.where(even, pltpu.roll(tj, LANES - 1, axis=1), pltpu.roll(tj, 1, axis=1))
            outs.append(tj * cos + partner * sin)
        return outs

    q = jnp.dot(h, w_ref[:, OFF_Q:OFF_K], preferred_element_type=F32)
    qr = head_norm_rope(q, qg_ref[...], seg_ref[...])
    scale = HEAD_DIM ** -0.5 * math.log2(math.e)
    for j, qj in enumerate(qr):
        qj = (qj * scale).astype(BF16)
        q_ref[2 * j] = qj[:, :HEAD_DIM]
        q_ref[2 * j + 1] = qj[:, HEAD_DIM:]

    kv = jnp.dot(h, w_ref[:, OFF_K:OFF_R], preferred_element_type=F32)
    kr = head_norm_rope(kv[:, :KV_WIDTH], kg_ref[...], seg_ref[:KV_WIDTH, :KV_WIDTH])[0].astype(BF16)
    k_ref[0] = kr[:, :HEAD_DIM]
    k_ref[1] = kr[:, HEAD_DIM:]
    vv = kv[:, KV_WIDTH:].astype(BF16)
    ones = jnp.ones((tm, HEAD_DIM), BF16)
    v_ref[0] = jnp.concatenate([vv[:, :HEAD_DIM], ones], axis=1)
    v_ref[1] = jnp.concatenate([vv[:, HEAD_DIM:], ones], axis=1)

    step = 640
    for c in range(SHIFT_COLS // step):
        p_ref[:, c * step:(c + 1) * step] = jnp.dot(
            h, w_ref[:, OFF_R + c * step:OFF_R + (c + 1) * step], preferred_element_type=F32)
    step = 512
    for c in range(2 * D_MODEL // step):
        gt = jnp.dot(h, w_ref[:, OFF_GATE + c * step:OFF_GATE + (c + 1) * step], preferred_element_type=F32)
        gate_ref[:, c * step:(c + 1) * step] = _sigmoid(gt).astype(BF16)


def _in_proj(x, g, w_bf, qg, kg, cos, sin, seg, seq_len):
    n = x.shape[0]
    tm = TM_IN
    tiles_per_seq = seq_len // tm
    tok = lambda i: (i, 0)
    return pl.pallas_call(
        _in_proj_kernel,
        grid=(n // tm,),
        in_specs=[
            pl.BlockSpec((tm, D_MODEL), tok),
            _full((1, D_MODEL)),
            _full((D_MODEL, IN_COLS)),
            _full((1, ATTN_WIDTH)),
            _full((1, KV_WIDTH)),
            pl.BlockSpec((tm, LANES), lambda i: (i % tiles_per_seq, 0)),
            pl.BlockSpec((tm, LANES), lambda i: (i % tiles_per_seq, 0)),
            _full((ATTN_WIDTH, ATTN_WIDTH)),
        ],
        out_specs=[
            pl.BlockSpec((ATTN_HEADS, tm, HEAD_DIM), lambda i: (0, i, 0)),
            pl.BlockSpec((ATTN_KV_HEADS, tm, HEAD_DIM), lambda i: (0, i, 0)),
            pl.BlockSpec((ATTN_KV_HEADS, tm, LANES), lambda i: (0, i, 0)),
            pl.BlockSpec((tm, SHIFT_COLS), tok),
            pl.BlockSpec((tm, 2 * D_MODEL), tok),
        ],
        out_shape=[
            jax.ShapeDtypeStruct((ATTN_HEADS, n, HEAD_DIM), BF16),
            jax.ShapeDtypeStruct((ATTN_KV_HEADS, n, HEAD_DIM), BF16),
            jax.ShapeDtypeStruct((ATTN_KV_HEADS, n, LANES), BF16),
            jax.ShapeDtypeStruct((n, SHIFT_COLS), F32),
            jax.ShapeDtypeStruct((n, 2 * D_MODEL), BF16),
        ],
        compiler_params=_cparams(("parallel",)),
        name="in_proj",
    )(x, g, w_bf, qg, kg, cos, sin, seg)


def _attn_kernel(q_ref, k_ref, v_ref, o_ref, m_sc, acc_sc):
    ki = pl.program_id(3)

    @pl.when(ki == 0)
    def _():
        m_sc[...] = jnp.full(m_sc.shape, -jnp.inf, F32)
        acc_sc[...] = jnp.zeros(acc_sc.shape, F32)

    g, tq, hd = q_ref.shape
    k = k_ref[...]
    v = v_ref[...]
    scores = lambda i: lax.dot_general(q_ref[i], k, (((1,), (1,)), ((), ())), preferred_element_type=F32)

    s = scores(0)
    for i in range(g):
        s_next = scores(i + 1) if i + 1 < g else None
        rows = slice(i * tq, (i + 1) * tq)
        m_prev = m_sc[rows]
        m_next = jnp.maximum(m_prev, jnp.max(s, axis=1, keepdims=True))
        alpha = jnp.exp2(m_prev - m_next)
        m_wide = jnp.concatenate([m_next] * (s.shape[1] // LANES), axis=1)
        p = jnp.exp2((s - m_wide).astype(BF16))
        acc_sc[rows] = acc_sc[rows] * alpha + jnp.dot(p, v, preferred_element_type=F32)
        m_sc[rows] = m_next
        s = s_next

    @pl.when(ki == pl.num_programs(3) - 1)
    def _():
        acc = acc_sc[...]
        o = acc / pltpu.roll(acc, HEAD_DIM, axis=1)
        o_ref[...] = jnp.concatenate([o[i * tq:(i + 1) * tq, :hd] for i in range(g)], axis=1).astype(o_ref.dtype)


def _attention(q, k, v, batch, seq_len):
    n = q.shape[1]
    nq = seq_len // TQ
    nk = seq_len // TK
    rows = ATTN_GROUP * TQ
    return pl.pallas_call(
        _attn_kernel,
        grid=(batch, ATTN_KV_HEADS, nq, nk),
        in_specs=[
            pl.BlockSpec((ATTN_GROUP, TQ, HEAD_DIM), lambda b, h, qi, ki: (h, b * nq + qi, 0)),
            pl.BlockSpec((None, TK, HEAD_DIM), lambda b, h, qi, ki: (h, b * nk + ki, 0)),
            pl.BlockSpec((None, TK, LANES), lambda b, h, qi, ki: (h, b * nk + ki, 0)),
        ],
        out_specs=pl.BlockSpec((TQ, ATTN_GROUP * HEAD_DIM), lambda b, h, qi, ki: (b * nq + qi, h)),
        out_shape=jax.ShapeDtypeStruct((n, ATTN_WIDTH), BF16),
        scratch_shapes=[pltpu.VMEM((rows, LANES), F32), pltpu.VMEM((rows, LANES), F32)],
        compiler_params=_cparams(("parallel", "parallel", "parallel", "arbitrary")),
        name="attention",
    )(q, k, v)


def _prep_kernel(p_ref, pprev_ref, pnext_ref, mu_ref, wup_ref, aup_ref, gup_ref, w0_ref, a0_ref,
                 kkw_ref, ka_ref, rk_ref, seg_ref,
                 r_o, v_o, kk_o, ld0_o, ld1_o, kd0_o, kd1_o, b0_o, b1_o, bonus_o, g_o, *, tiles_per_seq):
    pos = pl.program_id(0) % tiles_per_seq
    p = p_ref[...]
    tm = p.shape[0]
    row = lax.broadcasted_iota(jnp.int32, (tm, 1), 0)
    prev_row = jnp.where(pos == 0, 0.0, pprev_ref[7:8, :])
    next_row = jnp.where(pos == tiles_per_seq - 1, 0.0, pnext_ref[0:1, :])
    prev = jnp.where(row == 0, prev_row, pltpu.roll(p, 1, axis=0))
    nxt = jnp.where(row == tm - 1, next_row, pltpu.roll(p, tm - 1, axis=0))
    z = p + (0.5 * (prev + nxt) - p) * mu_ref[...]

    w = RWKV_WIDTH
    r = z[:, 0:w]
    k = z[:, w:2 * w]
    v = z[:, 2 * w:3 * w]
    wd = z[:, 3 * w:3 * w + 2 * DECAY_LORA]
    ad = z[:, 3 * w + 2 * DECAY_LORA:3 * w + 2 * DECAY_LORA + 2 * AAA_LORA]
    gd = z[:, 3 * w + 2 * DECAY_LORA + 2 * AAA_LORA:]

    wpre = w0_ref[...] + jnp.dot(jnp.tanh(wd).astype(BF16), wup_ref[...], preferred_element_type=F32)
    ld = -math.exp(-0.5) * _sigmoid(wpre)
    a = _sigmoid(a0_ref[...] + jnp.dot(ad.astype(BF16), aup_ref[...], preferred_element_type=F32))
    g = jnp.dot(_sigmoid(gd).astype(BF16), gup_ref[...], preferred_element_type=F32)

    seg = seg_ref[...]
    kk = k * kkw_ref[...]
    kk = kk / jnp.maximum(jnp.sqrt(_seg_dot2(kk * kk, seg)), 1e-12)
    ka = ka_ref[...]
    kd0 = k * (1.0 + (a[:, :w] - 1.0) * ka)
    kd1 = k * (1.0 + (a[:, w:] - 1.0) * ka)
    bonus = _seg_dot2(r * (0.5 * (kd0 + kd1)) * rk_ref[...], seg) * v

    ld0_o[...] = ld[:, :w]
    ld1_o[...] = ld[:, w:]
    for ref, val in ((r_o, r), (v_o, v), (kk_o, kk), (kd0_o, kd0), (kd1_o, kd1), (b0_o, kk * a[:, :w]),
                     (b1_o, kk * a[:, w:]), (bonus_o, bonus), (g_o, g)):
        ref[...] = val.astype(ref.dtype)


def _rwkv_prep(p, mu, wup, aup, gup, w0, a0, kkw, ka, rk, seg, seq_len):
    n = p.shape[0]
    tm = TM_PREP
    tiles_per_seq = seq_len // tm
    rows8 = tm // 8
    last8 = n // 8 - 1
    tok = lambda i: (i, 0)
    w = RWKV_WIDTH
    out_spec = pl.BlockSpec((tm, w), tok)
    return pl.pallas_call(
        functools.partial(_prep_kernel, tiles_per_seq=tiles_per_seq),
        grid=(n // tm,),
        in_specs=[
            pl.BlockSpec((tm, SHIFT_COLS), tok),
            pl.BlockSpec((8, SHIFT_COLS), lambda i: (jnp.maximum(i * rows8 - 1, 0), 0)),
            pl.BlockSpec((8, SHIFT_COLS), lambda i: (jnp.minimum((i + 1) * rows8, last8), 0)),
            _full((1, SHIFT_COLS)),
            _full((2 * DECAY_LORA, 2 * w)),
            _full((2 * AAA_LORA, 2 * w)),
            _full((GATE_LORA, w)),
            _full((1, 2 * w)),
            _full((1, 2 * w)),
            _full((1, w)),
            _full((1, w)),
            _full((1, w)),
            _full((w, w)),
        ],
        out_specs=[out_spec] * 11,
        out_shape=[jax.ShapeDtypeStruct((n, w), F32 if i in (3, 4) else BF16) for i in range(11)],
        compiler_params=_cparams(("parallel",)),
        name="rwkv_prep",
    )(p, p, p, mu, wup, aup, gup, w0, a0, kkw, ka, rk, seg)


def _mm(a, b):
    return jnp.dot(a, b, preferred_element_type=F32)


def _mm_nt(a, b):
    return lax.dot_general(a, b, (((1,), (1,)), ((), ())), preferred_element_type=F32)


def _mm_tn(a, b):
    return lax.dot_general(a, b, (((0,), (0,)), ((), ())), preferred_element_type=F32)


def _cumsum_rows(tri, x):
    hi = x.astype(BF16)
    r1 = x - hi.astype(F32)
    mid = r1.astype(BF16)
    lo = (r1 - mid.astype(F32)).astype(BF16)
    d = lambda y: jnp.dot(tri, y, preferred_element_type=F32)
    return d(hi) + d(mid) + d(lo)


def _chunk_operands(r, v, kk, ld, kd, b, reverse):
    c = r.shape[0]
    ri = lax.broadcasted_iota(jnp.int32, (c, c), 0)
    ci = lax.broadcasted_iota(jnp.int32, (c, c), 1)
    tri = ((ci >= ri) if reverse else (ci <= ri)).astype(BF16)
    gam = _cumsum_rows(tri, ld)
    g_end = gam[0:1, :] if reverse else gam[c - 1:c, :]
    ginv = jnp.exp(-gam)
    dec = jnp.exp(g_end - gam)
    ops = {
        "at": -kk * jnp.exp(gam - ld), "bt": b * ginv, "kt": kd * ginv, "rt": r * jnp.exp(gam),
        "bh": b * dec, "kh": kd * dec, "v": v,
    }
    return {k: x.astype(BF16) for k, x in ops.items()}, jnp.exp(g_end)


def _scan_block(chunk_refs, order, s_refs, y_refs):
    c = CHUNK
    n_pairs = RWKV_WIDTH // LANES
    lane = lax.broadcasted_iota(jnp.int32, (c, LANES), 1)
    row = lax.broadcasted_iota(jnp.int32, (c, LANES), 0)
    lo_half = lane < RWKV_HEAD
    s_idx = lane % c
    eye2 = (s_idx == row).astype(F32)
    masks = [((s_idx > row), (s_idx >= row)) if rev else ((s_idx < row), (s_idx <= row)) for rev in (False, True)]
    bd_r = lax.broadcasted_iota(jnp.int32, (LANES, LANES), 0) // RWKV_HEAD
    bd_c = lax.broadcasted_iota(jnp.int32, (LANES, LANES), 1) // RWKV_HEAD
    blockdiag = bd_r == bd_c

    def sm(x):
        x = x.astype(BF16)
        zero = jnp.zeros_like(x)
        return jnp.concatenate([jnp.where(lo_half, x, zero), jnp.where(lo_half, zero, x)], axis=0)

    chains = [(d, j, pr) for d in range(2) for j in order[d] for pr in range(n_pairs)]
    operands = {}
    op = lambda name, ch: operands[ch[:2]][0][name][:, ch[2] * LANES:(ch[2] + 1) * LANES]

    for d in range(2):
        for j in order[d]:
            operands[(d, j)] = _chunk_operands(*chunk_refs(d, j), reverse=bool(d))
    aa = [_mm_nt(jnp.concatenate([op("at", ch), op("rt", ch)], axis=0),
                 jnp.concatenate([sm(op("bt", ch)), sm(op("kt", ch))], axis=0)) for ch in chains]
    strict = [masks[ch[0]][0] for ch in chains]
    incl = [masks[ch[0]][1] for ch in chains]
    a_ak = [jnp.where(m, x[:c, LANES:], 0.0).astype(BF16) for m, x in zip(strict, aa)]
    a_rb = [jnp.where(m, x[c:, :LANES], 0.0).astype(BF16) for m, x in zip(incl, aa)]
    a_rk = [jnp.where(m, x[c:, LANES:], 0.0).astype(BF16) for m, x in zip(incl, aa)]
    avk = [_mm(jnp.concatenate([x, z], axis=0), sm(op("v", ch))) for x, z, ch in zip(a_ak, a_rk, chains)]
    pw = [jnp.where(m, x[:c, :LANES], 0.0) for m, x in zip(strict, aa)]
    t_inv = [eye2 + x for x in pw]
    pw = [_mm(x.astype(BF16), sm(x)) for x in pw]
    for _ in range(int(math.log2(c)) - 2):
        tp = [_mm(jnp.concatenate([t.astype(BF16), x.astype(BF16)], axis=0), sm(x)) for t, x in zip(t_inv, pw)]
        t_inv = [t + z[:c] for t, z in zip(t_inv, tp)]
        pw = [z[c:] for z in tp]
    t_inv = [t + _mm(t.astype(BF16), sm(x)) for t, x in zip(t_inv, pw)]
    wx = [_mm(t.astype(BF16), jnp.concatenate([sm(op("at", ch)), sm(x[:c])], axis=1))
          for t, x, ch in zip(t_inv, avk, chains)]
    pre = {ch: (w[:, :LANES].astype(BF16), w[:, LANES:], a, x[c:]) for ch, w, a, x in zip(chains, wx, a_rb, avk)}

    state = {(d, pr): s_refs[d][pr] for d in range(2) for pr in range(n_pairs)}
    for step in range(len(order[0])):
        tail = [(d, order[d][step], pr) for d in range(2) for pr in range(n_pairs)]
        s0 = [state[(d, pr)] for d, _, pr in tail]
        us = [_mm_nt(jnp.concatenate([pre[ch][0], op("rt", ch)], axis=0), s.astype(BF16)) for ch, s in zip(tail, s0)]
        u16 = [(x[:c] + pre[ch][1]).astype(BF16) for x, ch in zip(us, tail)]
        upd = [_mm_tn(jnp.concatenate([u, op("v", ch)], axis=0),
                      jnp.concatenate([op("bh", ch), op("kh", ch)], axis=0)) for u, ch in zip(u16, tail)]
        for ch, s, x in zip(tail, s0, upd):
            g_c = operands[ch[:2]][1][:, ch[2] * LANES:(ch[2] + 1) * LANES]
            state[(ch[0], ch[2])] = s * g_c + jnp.where(blockdiag, x, 0.0)
        y = [x[c:] + pre[ch][3] + _mm(pre[ch][2], sm(u)) for x, u, ch in zip(us, u16, tail)]
        for (d, j, pr), x in zip(tail, y):
            y_refs[d][j * c:(j + 1) * c, pr * LANES:(pr + 1) * LANES] = x
    for (d, pr), s in state.items():
        s_refs[d][pr] = s


def _scan_kernel(rf, vf, kkf, ldf, kdf, bf, rb, vb, kkb, ldb, kdb, bb, yf_ref, yb_ref, s_sc):
    @pl.when(pl.program_id(1) == 0)
    def _():
        s_sc[...] = jnp.zeros(s_sc.shape, F32)

    c = CHUNK
    n_chunks = rf.shape[0] // c
    refs = ((rf, vf, kkf, ldf, kdf, bf), (rb, vb, kkb, ldb, kdb, bb))
    chunk_refs = lambda d, j: tuple(x[j * c:(j + 1) * c, :].astype(F32) for x in refs[d])
    order = (list(range(n_chunks)), list(range(n_chunks - 1, -1, -1)))
    _scan_block(chunk_refs, order, (s_sc.at[0], s_sc.at[1]), (yf_ref, yb_ref))


def _rwkv_scan(r, v, kk, ld0, ld1, kd0, kd1, b0, b1, batch, seq_len):
    n = r.shape[0]
    rows = SCAN_CHUNKS * CHUNK
    nb = seq_len // rows
    w = RWKV_WIDTH
    fwd = pl.BlockSpec((rows, w), lambda b, i: (b * nb + i, 0))
    bwd = pl.BlockSpec((rows, w), lambda b, i: (b * nb + nb - 1 - i, 0))
    return pl.pallas_call(
        _scan_kernel,
        grid=(batch, nb),
        in_specs=[fwd] * 6 + [bwd] * 6,
        out_specs=[fwd, bwd],
        out_shape=[jax.ShapeDtypeStruct((n, w), F32)] * 2,
        scratch_shapes=[pltpu.VMEM((2, w // LANES, LANES, LANES), F32)],
        compiler_params=_cparams(("parallel", "arbitrary")),
        name="rwkv_scan",
    )(r, v, kk, ld0, kd0, b0, r, v, kk, ld1, kd1, b1)


def _merge_kernel(x_ref, attn_ref, yf_ref, yb_ref, bonus_ref, g_ref, gate_ref, lnxw_ref, lnxb_ref,
                  seg_ref, wb0_ref, wb1_ref, wout_ref, gpost_ref, o_ref):
    seg = seg_ref[...]
    y = yf_ref[...] + yb_ref[...]
    mu = _seg_dot2(y, seg) * (1.0 / RWKV_HEAD)
    yc = y - mu
    var = _seg_dot2(yc * yc, seg) * (1.0 / RWKV_HEAD)
    yn = yc * lax.rsqrt(var + LNX_EPS) * lnxw_ref[...] + lnxb_ref[...]
    rw = ((yn + bonus_ref[...].astype(F32)) * g_ref[...].astype(F32)).astype(BF16)
    br0 = jnp.dot(attn_ref[...], wb0_ref[...], preferred_element_type=F32)
    br1 = jnp.dot(rw, wb1_ref[...], preferred_element_type=F32)
    gates = gate_ref[...]
    merged = gates[:, :D_MODEL].astype(F32) * br0 + gates[:, D_MODEL:].astype(F32) * br1
    mo = jnp.dot(merged.astype(BF16), wout_ref[...], preferred_element_type=F32)
    ms = jnp.mean(mo * mo, axis=-1, keepdims=True)
    o_ref[...] = x_ref[...] + mo * lax.rsqrt(ms + NORM_EPS) * gpost_ref[...]


def _merge(x, attn, yf, yb, bonus, g, gates, lnxw, lnxb, seg, wb0, wb1, wout, gpost):
    n = x.shape[0]
    tm = TM_MERGE
    w = RWKV_WIDTH
    tok = lambda i: (i, 0)
    return pl.pallas_call(
        _merge_kernel,
        grid=(n // tm,),
        in_specs=[
            pl.BlockSpec((tm, D_MODEL), tok),
            pl.BlockSpec((tm, ATTN_WIDTH), tok),
            pl.BlockSpec((tm, w), tok),
            pl.BlockSpec((tm, w), tok),
            pl.BlockSpec((tm, w), tok),
            pl.BlockSpec((tm, w), tok),
            pl.BlockSpec((tm, 2 * D_MODEL), tok),
            _full((1, w)),
            _full((1, w)),
            _full((w, w)),
            _full((ATTN_WIDTH, D_MODEL)),
            _full((w, D_MODEL)),
            _full((D_MODEL, D_MODEL)),
            _full((1, D_MODEL)),
        ],
        out_specs=pl.BlockSpec((tm, D_MODEL), tok),
        out_shape=jax.ShapeDtypeStruct((n, D_MODEL), F32),
        compiler_params=_cparams(("parallel",)),
        name="merge",
    )(x, attn, yf, yb, bonus, g, gates, lnxw, lnxb, seg, wb0, wb1, wout, gpost)


def _ffn_kernel(x_ref, gpre_ref, wg_ref, wu_ref, wo_ref, gpost_ref, o_ref, h_sc, acc_sc):
    j = pl.program_id(1)

    @pl.when(j == 0)
    def _():
        x = x_ref[...]
        ms = jnp.mean(x * x, axis=-1, keepdims=True)
        h_sc[...] = (x * lax.rsqrt(ms + NORM_EPS) * gpre_ref[...]).astype(BF16)
        acc_sc[...] = jnp.zeros(acc_sc.shape, F32)

    h = h_sc[...]
    gg = jnp.dot(h, wg_ref[...], preferred_element_type=F32)
    uu = jnp.dot(h, wu_ref[...], preferred_element_type=F32)
    f = (gg * _sigmoid(gg) * uu).astype(BF16)
    acc_sc[...] += jnp.dot(f, wo_ref[...], preferred_element_type=F32)

    @pl.when(j == pl.num_programs(1) - 1)
    def _():
        acc = acc_sc[...]
        ms = jnp.mean(acc * acc, axis=-1, keepdims=True)
        o_ref[...] = x_ref[...] + acc * lax.rsqrt(ms + NORM_EPS) * gpost_ref[...]


def _ffn(x, gpre, w_in_bf, w_out_bf, gpost):
    n = x.shape[0]
    tm, tf = TM_FFN, TF_FFN
    nf = D_FF // tf
    return pl.pallas_call(
        _ffn_kernel,
        grid=(n // tm, nf),
        in_specs=[
            pl.BlockSpec((tm, D_MODEL), lambda i, j: (i, 0)),
            _full((1, D_MODEL)),
            pl.BlockSpec((D_MODEL, tf), lambda i, j: (0, j)),
            pl.BlockSpec((D_MODEL, tf), lambda i, j: (0, nf + j)),
            pl.BlockSpec((tf, D_MODEL), lambda i, j: (j, 0)),
            _full((1, D_MODEL)),
        ],
        out_specs=pl.BlockSpec((tm, D_MODEL), lambda i, j: (i, 0)),
        out_shape=jax.ShapeDtypeStruct((n, D_MODEL), F32),
        scratch_shapes=[pltpu.VMEM((tm, D_MODEL), BF16), pltpu.VMEM((tm, D_MODEL), F32)],
        compiler_params=_cparams(("parallel", "arbitrary")),
        name="ffn",
    )(x, gpre, w_in_bf, w_in_bf, w_out_bf, gpost)


def _rope_tables(n):
    rows = n // GRID_W
    row = jnp.broadcast_to(jnp.arange(rows)[:, None], (rows, GRID_W)).reshape(-1).astype(F32)
    col = jnp.broadcast_to(jnp.arange(GRID_W)[None, :], (rows, GRID_W)).reshape(-1).astype(F32)
    nf = HEAD_DIM // 4
    inv = ROPE_THETA ** (-jnp.arange(nf, dtype=F32) / nf)
    ang = jnp.concatenate([row[:, None] * inv, col[:, None] * inv], axis=-1)
    cos = jnp.repeat(jnp.cos(ang), 2, axis=-1)
    sin = jnp.repeat(jnp.sin(ang), 2, axis=-1)
    sign = jnp.tile(jnp.array([-1.0, 1.0], F32), HEAD_DIM // 2)
    reps = LANES // HEAD_DIM
    return jnp.tile(cos, (1, reps)), jnp.tile(sin * sign, (1, reps))


def _block_diag2(m):
    z = jnp.zeros_like(m[0])
    return jnp.concatenate([jnp.concatenate([m[0], z], axis=1), jnp.concatenate([z, m[1]], axis=1)], axis=0)


def _trunk(x3, prm):
    batch, seq_len, _ = x3.shape
    for tile in (TM_IN, TQ, TK, TM_PREP, SCAN_CHUNKS * CHUNK, TM_MERGE, TM_FFN, GRID_W):
        assert seq_len % tile == 0, (seq_len, tile)
    n = batch * seq_len
    x = x3.reshape(n, D_MODEL)
    cos, sin = _rope_tables(seq_len)
    q, k, v, p, gates = _in_proj(x, prm["ln_mix_pre"], prm["w_in"], prm["q_gain"], prm["k_gain"],
                                 cos, sin, prm["seg"], seq_len)
    attn = _attention(q, k, v, batch, seq_len)
    r, vv, kk, ld0, ld1, kd0, kd1, b0, b1, bonus, g = _rwkv_prep(
        p, prm["shift_mu"], prm["w_up"], prm["a_up"], prm["g_up"], prm["w0"], prm["a0"],
        prm["k_k"], prm["k_a"], prm["r_k"], prm["seg"], seq_len)
    yf, yb = _rwkv_scan(r, vv, kk, ld0, ld1, kd0, kd1, b0, b1, batch, seq_len)
    x1 = _merge(x, attn, yf, yb, bonus, g, gates, prm["lnx_w"], prm["lnx_b"], prm["seg"],
                prm["wb0"], prm["wb1"], prm["w_out"], prm["ln_mix_post"])
    x2 = _ffn(x1, prm["ln_ffn_pre"], prm["w_ffn_in"], prm["w_ffn_out"], prm["ln_ffn_post"])
    return x2.reshape(batch, seq_len, D_MODEL)


def _prepare_params(ln_mix_pre, ln_mix_post, ln_ffn_pre, ln_ffn_post, w_in, q_gain, k_gain, shift_mu,
                    w0, w_up, a0, a_up, g_up, k_k, k_a, r_k, lnx_w, lnx_b, w_branch, w_out, w_ffn_in, w_ffn_out):
    row = lambda t: t.reshape(1, -1).astype(F32)
    head = jnp.arange(RWKV_WIDTH) // RWKV_HEAD
    seg = (head[:, None] == head[None, :]).astype(BF16)
    return {
        "ln_mix_pre": row(ln_mix_pre), "ln_mix_post": row(ln_mix_post),
        "ln_ffn_pre": row(ln_ffn_pre), "ln_ffn_post": row(ln_ffn_post),
        "w_in": w_in.astype(BF16),
        "q_gain": row(jnp.tile(q_gain, ATTN_HEADS)), "k_gain": row(jnp.tile(k_gain, ATTN_KV_HEADS)),
        "shift_mu": row(shift_mu),
        "w0": row(w0), "a0": row(a0),
        "w_up": _block_diag2(w_up).astype(BF16), "a_up": _block_diag2(a_up).astype(BF16),
        "g_up": g_up.astype(BF16),
        "k_k": row(k_k), "k_a": row(k_a), "r_k": row(r_k),
        "lnx_w": row(lnx_w), "lnx_b": row(lnx_b),
        "wb0": w_branch[0].astype(BF16), "wb1": w_branch[1].astype(BF16),
        "w_out": w_out.astype(BF16),
        "w_ffn_in": w_ffn_in.astype(BF16), "w_ffn_out": w_ffn_out.astype(BF16),
        "seg": seg,
    }


def kernel(x_prompt, x_sample, ln_mix_pre, ln_mix_post, ln_ffn_pre, ln_ffn_post, w_in, q_gain, k_gain, shift_mu,
           w0, w_up, a0, a_up, g_up, k_k, k_a, r_k, lnx_w, lnx_b, w_branch, w_out, w_ffn_in, w_ffn_out):
    weights = (ln_mix_pre, ln_mix_post, ln_ffn_pre, ln_ffn_post, w_in, q_gain, k_gain, shift_mu,
               w0, w_up, a0, a_up, g_up, k_k, k_a, r_k, lnx_w, lnx_b, w_branch, w_out, w_ffn_in, w_ffn_out)
    assert all(t.shape[0] == 1 for t in weights), "single-layer stack expected"
    prm = _prepare_params(*(t[0] for t in weights))
    return (_trunk(x_prompt, prm), _trunk(x_sample, prm))
```
